```python
import jax, jax.numpy as jnp
from jax import lax
import numpy as np

D_MODEL = 2048
BATCH = 2
SEQ = 16384
DEPTH = 1

PLE_DIM = 256
CHUNK = 128
SGU_GROUPS = 8
SGU_GROUP_DIM = 128
SGU_WIDTH = SGU_GROUPS * SGU_GROUP_DIM
SB_HEADS = 8
SB_HEAD_DIM = 128
SB_WIDTH = SB_HEADS * SB_HEAD_DIM
Q_BLOCK = 128
FFN_HIDDEN = ((8 * D_MODEL + 3 * 256 - 1) // (3 * 256)) * 256
EPS = 1e-6
IN_COLS = 2 * SGU_WIDTH + 3 * SB_WIDTH + 2 * D_MODEL

kernel_name = "hybrid_sgu_stickbreaking_block"


def rmsnorm(x, g):
    xf = x.astype(jnp.float32)
    y = xf * lax.rsqrt(jnp.mean(xf * xf, axis=-1, keepdims=True) + EPS)
    return (y * g.astype(jnp.float32)).astype(x.dtype)


def chunked_spatial_gating(u, v, w_s, b_s):
    B, S, G, C = v.shape
    tri = jnp.tril(jnp.ones((CHUNK, CHUNK), dtype=w_s.dtype))
    ws = w_s * tri[None]
    vc = v.reshape(B, S // CHUNK, CHUNK, G, C)
    mixed = jnp.einsum('gts,bnsgc->bntgc', ws, vc)
    mixed = mixed + jnp.transpose(b_s)[None, None, :, :, None]
    return u * mixed.reshape(B, S, G, C)


def stick_breaking_attention(q, k, v):
    B, S, H, Dh = q.shape
    nb = S // Q_BLOCK
    scale = Dh ** -0.5
    qf = jnp.transpose(q, (0, 2, 1, 3))
    kf = jnp.transpose(k, (0, 2, 1, 3))
    vf = jnp.transpose(v, (0, 2, 1, 3))
    key_pos = jnp.arange(S, dtype=jnp.int32)

    def one_block(blk):
        start = blk * Q_BLOCK
        qi = lax.dynamic_slice_in_dim(qf, start, Q_BLOCK, axis=2)
        z = jnp.einsum('bhqd,bhkd->bhqk', qi, kf).astype(jnp.float32) * scale
        q_pos = start + jnp.arange(Q_BLOCK, dtype=jnp.int32)
        strict = key_pos[None, :] < q_pos[:, None]
        log_beta = jax.nn.log_sigmoid(z)
        log_1m = jnp.where(strict, jax.nn.log_sigmoid(-z), jnp.zeros_like(z))
        tail = lax.cumsum(log_1m, axis=3, reverse=True) - log_1m
        w = jnp.where(strict, jnp.exp(log_beta + tail), jnp.zeros_like(z))
        return jnp.einsum('bhqk,bhkd->bhqd', w.astype(vf.dtype), vf)

    out = lax.map(one_block, jnp.arange(nb, dtype=jnp.int32))
    out = jnp.transpose(out, (1, 0, 3, 2, 4))
    return out.reshape(B, S, H, Dh)


def setup_inputs(seed: int = 0) -> dict:
    key = jax.random.key(seed)
    ks = jax.random.split(key, 20)
    f = jnp.float32

    def nrm(k, shape, fan_in):
        return jax.random.normal(k, shape, f) * (fan_in ** -0.5)

    def gain(k, shape):
        return 1.0 + 0.05 * jax.random.normal(k, shape, f)

    return {
        "x": jax.random.normal(ks[0], (BATCH, SEQ, D_MODEL), f),
        "p": jax.random.normal(ks[1], (DEPTH, BATCH, SEQ, PLE_DIM), f),
        "attn_norm_g": gain(ks[2], (DEPTH, D_MODEL)),
        "w_in": nrm(ks[3], (DEPTH, D_MODEL, IN_COLS), D_MODEL),
        "sgu_norm_g": gain(ks[4], (DEPTH, SGU_GROUPS, SGU_GROUP_DIM)),
        "w_s": nrm(ks[5], (DEPTH, SGU_GROUPS, CHUNK, CHUNK), CHUNK),
        "b_s": 1.0 + 0.1 * jax.random.normal(ks[6], (DEPTH, SGU_GROUPS, CHUNK), f),
        "q_norm_g": gain(ks[7], (DEPTH, SB_HEAD_DIM)),
        "k_norm_g": gain(ks[8], (DEPTH, SB_HEAD_DIM)),
        "w_up_a": nrm(ks[9], (DEPTH, SGU_WIDTH, D_MODEL), SGU_WIDTH),
        "w_up_b": nrm(ks[10], (DEPTH, SB_WIDTH, D_MODEL), SB_WIDTH),
        "w_o": nrm(ks[11], (DEPTH, D_MODEL, D_MODEL), D_MODEL),
        "ffn_norm_g": gain(ks[12], (DEPTH, D_MODEL)),
        "w_ffn_in": nrm(ks[13], (DEPTH, D_MODEL, 2 * FFN_HIDDEN), D_MODEL),
        "w_ffn_out": nrm(ks[14], (DEPTH, FFN_HIDDEN, D_MODEL), FFN_HIDDEN),
        "ple_norm_g": gain(ks[15], (DEPTH, D_MODEL)),
        "w_ple_gate": nrm(ks[16], (DEPTH, D_MODEL, D_MODEL), D_MODEL),
        "w_ple": nrm(ks[17], (DEPTH, PLE_DIM, D_MODEL), PLE_DIM),
    }


def reference(x, p, attn_norm_g, w_in, sgu_norm_g, w_s, b_s, q_norm_g, k_norm_g,
              w_up_a, w_up_b, w_o, ffn_norm_g, w_ffn_in, w_ffn_out,
              ple_norm_g, w_ple_gate, w_ple):
    B, S, D = x.shape
    c0 = SGU_WIDTH
    c1 = c0 + SGU_WIDTH
    c2 = c1 + SB_WIDTH
    c3 = c2 + SB_WIDTH
    c4 = c3 + SB_WIDTH
    c5 = c4 + D_MODEL
    for i in range(DEPTH):
        h = rmsnorm(x, attn_norm_g[i])
        proj = h @ w_in[i]
        u = proj[..., :c0]
        v_sgu = proj[..., c0:c1]
        q = proj[..., c1:c2]
        k = proj[..., c2:c3]
        v_att = proj[..., c3:c4]
        g_a = proj[..., c4:c5]
        g_b = proj[..., c5:]

        u = jax.nn.gelu(u).reshape(B, S, SGU_GROUPS, SGU_GROUP_DIM)
        v_sgu = rmsnorm(jax.nn.gelu(v_sgu).reshape(B, S, SGU_GROUPS, SGU_GROUP_DIM), sgu_norm_g[i])
        y_a = chunked_spatial_gating(u, v_sgu, w_s[i], b_s[i]).reshape(B, S, SGU_WIDTH)

        q = rmsnorm(q.reshape(B, S, SB_HEADS, SB_HEAD_DIM), q_norm_g[i])
        k = rmsnorm(k.reshape(B, S, SB_HEADS, SB_HEAD_DIM), k_norm_g[i])
        v_att = v_att.reshape(B, S, SB_HEADS, SB_HEAD_DIM)
        y_b = stick_breaking_attention(q, k, v_att).reshape(B, S, SB_WIDTH)

        merged = jax.nn.sigmoid(g_a) * (y_a @ w_up_a[i]) + jax.nn.sigmoid(g_b) * (y_b @ w_up_b[i])
        x = x + merged @ w_o[i]

        h = rmsnorm(x, ffn_norm_g[i])
        hid = h @ w_ffn_in[i]
        gate = hid[..., :FFN_HIDDEN]
        up = hid[..., FFN_HIDDEN:]
        x = x + (jax.nn.silu(gate) * up) @ w_ffn_out[i]

        ple = p[i] @ w_ple[i]
        x = x + jax.nn.sigmoid(rmsnorm(x, ple_norm_g[i]) @ w_ple_gate[i]) * ple
    return x
```

```python
import functools

import jax
import jax.numpy as jnp
from jax import lax
from jax.experimental import pallas as pl
from jax.experimental.pallas import tpu as pltpu

F32 = jnp.float32
BF16 = jnp.bfloat16

EPS = 1e-6
GROUP = 128
N_GROUPS = 8
WIDTH = GROUP * N_GROUPS

V7X_VMEM_BYTES = 64 * 1024 * 1024
VMEM_LIMIT = V7X_VMEM_BYTES - 8 * 1024 * 1024

EXP_UNDERFLOW = -104.0


def _tile(n, want):
    if n <= want:
        return n
    t = (want // 128) * 128
    while t >= 128:
        if n % t == 0:
            return t
        t -= 128
    return n


def _params(sem):
    return pltpu.CompilerParams(dimension_semantics=sem, vmem_limit_bytes=VMEM_LIMIT)


def _dot(a, b):
    return jnp.dot(a, b, preferred_element_type=F32)


def _rms_rows(x, g):
    y = x * lax.rsqrt(jnp.mean(x * x, axis=-1, keepdims=True) + EPS)
    return y * g


def _rmsnorm_kernel(x_ref, g_ref, o_ref):
    o_ref[...] = _rms_rows(x_ref[...], g_ref[...]).astype(o_ref.dtype)


def _rmsnorm(x, g, tm):
    t, d = x.shape
    return pl.pallas_call(
        _rmsnorm_kernel,
        grid=(t // tm,),
        in_specs=[pl.BlockSpec((tm, d), lambda i: (i, 0)),
                  pl.BlockSpec((1, d), lambda i: (0, 0))],
        out_specs=pl.BlockSpec((tm, d), lambda i: (i, 0)),
        out_shape=jax.ShapeDtypeStruct((t, d), BF16),
        compiler_params=_params(("parallel",)),
        name="rmsnorm",
    )(x, g.reshape(1, d))


def _proj_kernel(h_ref, w_ref, gain_ref, o_ref, *, kind):
    acc = _dot(h_ref[...], w_ref[...])
    if kind == "gelu":
        o_ref[...] = jax.nn.gelu(acc).astype(o_ref.dtype)
    elif kind == "sigmoid":
        o_ref[...] = jax.nn.sigmoid(acc).astype(o_ref.dtype)
    elif kind == "none":
        o_ref[...] = acc.astype(o_ref.dtype)
    elif kind == "headnorm":
        gain = gain_ref[...]
        for c in range(acc.shape[1] // GROUP):
            sl = slice(c * GROUP, (c + 1) * GROUP)
            o_ref[:, sl] = _rms_rows(acc[:, sl], gain[:, sl]).astype(o_ref.dtype)
    else:
        raise ValueError(kind)


def _proj(h, w, col0, ncols, kind, gain, tm, tn, name):
    t, d = h.shape
    tn = _tile(ncols, tn)
    off = col0 // tn
    assert col0 % tn == 0 and ncols % tn == 0
    if gain is None:
        gain = jnp.ones((1, ncols), F32)
    return pl.pallas_call(
        functools.partial(_proj_kernel, kind=kind),
        grid=(t // tm, ncols // tn),
        in_specs=[pl.BlockSpec((tm, d), lambda i, j: (i, 0)),
                  pl.BlockSpec((d, tn), lambda i, j: (0, j + off)),
                  pl.BlockSpec((1, tn), lambda i, j: (0, j))],
        out_specs=pl.BlockSpec((tm, tn), lambda i, j: (i, j)),
        out_shape=jax.ShapeDtypeStruct((t, ncols), BF16),
        compiler_params=_params(("parallel", "parallel")),
        name=name,
    )(h, w, gain)


def _sgu_kernel(u_ref, v_ref, g_ref, ws_ref, b_ref, o_ref, *, tm):
    nchunk = tm // GROUP
    row = lax.broadcasted_iota(jnp.int32, (GROUP, GROUP), 0)
    col = lax.broadcasted_iota(jnp.int32, (GROUP, GROUP), 1)
    causal = row >= col
    for g in range(N_GROUPS):
        sl = slice(g * GROUP, (g + 1) * GROUP)
        vn = _rms_rows(v_ref[:, sl].astype(F32), g_ref[:, sl]).astype(BF16)
        ws = jnp.where(causal, ws_ref[g], 0.0).astype(BF16)
        vcat = jnp.concatenate([vn[c * GROUP:(c + 1) * GROUP, :] for c in range(nchunk)], axis=1)
        mixed = _dot(ws, vcat) + b_ref[:, g:g + 1]
        for c in range(nchunk):
            rows = slice(c * GROUP, (c + 1) * GROUP)
            u = u_ref[rows, sl].astype(F32)
            o_ref[rows, sl] = (u * mixed[:, c * GROUP:(c + 1) * GROUP]).astype(o_ref.dtype)


def _sgu(uv, sgu_g, w_s, b_s, tm):
    t = uv.shape[0]
    return pl.pallas_call(
        functools.partial(_sgu_kernel, tm=tm),
        grid=(t // tm,),
        in_specs=[pl.BlockSpec((tm, WIDTH), lambda i: (i, 0)),
                  pl.BlockSpec((tm, WIDTH), lambda i: (i, 1)),
                  pl.BlockSpec((1, WIDTH), lambda i: (0, 0)),
                  pl.BlockSpec((N_GROUPS, GROUP, GROUP), lambda i: (0, 0, 0)),
                  pl.BlockSpec((GROUP, N_GROUPS), lambda i: (0, 0))],
        out_specs=pl.BlockSpec((tm, WIDTH), lambda i: (i, 0)),
        out_shape=jax.ShapeDtypeStruct((t, WIDTH), BF16),
        compiler_params=_params(("parallel",)),
        name="sgu_mix",
    )(uv, uv, sgu_g.reshape(1, WIDTH), w_s, jnp.transpose(b_s))


def _attn_kernel(q_ref, k_ref, v_ref, o_ref, *, blk):
    i = pl.program_id(2)
    q = q_ref[...]
    row = lax.broadcasted_iota(jnp.int32, (blk, blk), 0)
    col = lax.broadcasted_iota(jnp.int32, (blk, blk), 1)
    strict = col < row
    later = strict.astype(BF16)

    def block(j, carry, acc, diagonal):
        start = pl.multiple_of(j * blk, blk)
        k = k_ref[pl.ds(start, blk), :]
        v = v_ref[pl.ds(start, blk), :]
        z = lax.dot_general(q, k, (((1,), (1,)), ((), ())), preferred_element_type=F32)
        log_beta = jnp.minimum(z, 0.0) - jnp.log1p(jnp.exp(-jnp.abs(z)))
        log_1m = log_beta - z
        if diagonal:
            log_1m = jnp.where(strict, log_1m, 0.0)
        hi = log_1m.astype(BF16)
        lo = (log_1m - hi.astype(F32)).astype(BF16)
        tail = _dot(hi, later) + _dot(lo, later) + carry
        w = jnp.exp(log_beta + tail)
        if diagonal:
            w = jnp.where(strict, w, 0.0)
        acc = acc + _dot(w.astype(BF16), v)
        carry = carry + jnp.sum(log_1m, axis=1, keepdims=True)
        return carry, acc

    carry, acc = block(i, jnp.zeros((blk, 1), F32), jnp.zeros((blk, GROUP), F32), True)

    def cond(state):
        j, carry, _ = state
        return jnp.logical_and(j >= 0, jnp.max(carry) > EXP_UNDERFLOW)

    def body(state):
        j, carry, acc = state
        carry, acc = block(j, carry, acc, False)
        return j - 1, carry, acc

    _, _, acc = lax.while_loop(cond, body, (i - 1, carry, acc))
    o_ref[...] = acc.astype(o_ref.dtype)


def _attention(qk, v, batch, seq, blk):
    t = qk.shape[0]
    nq = seq // blk
    return pl.pallas_call(
        functools.partial(_attn_kernel, blk=blk),
        grid=(batch, N_GROUPS, nq),
        in_specs=[pl.BlockSpec((blk, GROUP), lambda b, h, i: (b * nq + i, h)),
                  pl.BlockSpec((seq, GROUP), lambda b, h, i: (b, N_GROUPS + h)),
                  pl.BlockSpec((seq, GROUP), lambda b, h, i: (b, h))],
        out_specs=pl.BlockSpec((blk, GROUP), lambda b, h, i: (b * nq + i, h)),
        out_shape=jax.ShapeDtypeStruct((t, WIDTH), BF16),
        compiler_params=_params(("parallel", "parallel", "arbitrary")),
        name="stick_breaking_attention",
    )(qk, qk, v)


def _merge_kernel(ya_ref, yb_ref, wa_ref, wb_ref, ga_ref, gb_ref, o_ref):
    a = _dot(ya_ref[...], wa_ref[...])
    b = _dot(yb_ref[...], wb_ref[...])
    o_ref[...] = (ga_ref[...].astype(F32) * a + gb_ref[...].astype(F32) * b).astype(o_ref.dtype)


def _merge(ya, yb, wa, wb, gates, tm, tn):
    t = ya.shape[0]
    d = wa.shape[1]
    tn = _tile(d, tn)
    nb = d // tn
    return pl.pallas_call(
        _merge_kernel,
        grid=(t // tm, nb),
        in_specs=[pl.BlockSpec((tm, WIDTH), lambda i, j: (i, 0)),
                  pl.BlockSpec((tm, WIDTH), lambda i, j: (i, 0)),
                  pl.BlockSpec((WIDTH, tn), lambda i, j: (0, j)),
                  pl.BlockSpec((WIDTH, tn), lambda i, j: (0, j)),
                  pl.BlockSpec((tm, tn), lambda i, j: (i, j)),
                  pl.BlockSpec((tm, tn), lambda i, j: (i, j + nb))],
        out_specs=pl.BlockSpec((tm, tn), lambda i, j: (i, j)),
        out_shape=jax.ShapeDtypeStruct((t, d), BF16),
        compiler_params=_params(("parallel", "parallel")),
        name="gated_merge",
    )(ya, yb, wa, wb, gates, gates)


def _matmul_residual_kernel(a_ref, w_ref, r_ref, o_ref):
    o_ref[...] = r_ref[...] + _dot(a_ref[...], w_ref[...])


def _matmul_residual(a, w, r, tm, tn, name):
    t, k = a.shape
    d = w.shape[1]
    tn = _tile(d, tn)
    return pl.pallas_call(
        _matmul_residual_kernel,
        grid=(t // tm, d // tn),
        in_specs=[pl.BlockSpec((tm, k), lambda i, j: (i, 0)),
                  pl.BlockSpec((k, tn), lambda i, j: (0, j)),
                  pl.BlockSpec((tm, tn), lambda i, j: (i, j))],
        out_specs=pl.BlockSpec((tm, tn), lambda i, j: (i, j)),
        out_shape=jax.ShapeDtypeStruct((t, d), F32),
        compiler_params=_params(("parallel", "parallel")),
        name=name,
    )(a, w, r)


def _ffn_in_kernel(x_ref, g_ref, wg_ref, wu_ref, o_ref, h_ref):
    @pl.when(pl.program_id(1) == 0)
    def _():
        h_ref[...] = _rms_rows(x_ref[...], g_ref[...]).astype(h_ref.dtype)

    h = h_ref[...]
    gate = _dot(h, wg_ref[...])
    up = _dot(h, wu_ref[...])
    o_ref[...] = (jax.nn.silu(gate) * up).astype(o_ref.dtype)


def _ffn_in(x, g, w, hidden, tm, tn):
    t, d = x.shape
    tn = _tile(hidden, tn)
    nb = hidden // tn
    return pl.pallas_call(
        _ffn_in_kernel,
        grid=(t // tm, nb),
        in_specs=[pl.BlockSpec((tm, d), lambda i, j: (i, 0)),
                  pl.BlockSpec((1, d), lambda i, j: (0, 0)),
                  pl.BlockSpec((d, tn), lambda i, j: (0, j)),
                  pl.BlockSpec((d, tn), lambda i, j: (0, j + nb))],
        out_specs=pl.BlockSpec((tm, tn), lambda i, j: (i, j)),
        out_shape=jax.ShapeDtypeStruct((t, hidden), BF16),
        scratch_shapes=[pltpu.VMEM((tm, d), BF16)],
        compiler_params=_params(("parallel", "arbitrary")),
        name="swiglu_in",
    )(x, g.reshape(1, d), w, w)


def _ple_kernel(x_ref, g_ref, wg_ref, p_ref, wp_ref, r_ref, o_ref, h_ref):
    @pl.when(pl.program_id(1) == 0)
    def _():
        h_ref[...] = _rms_rows(x_ref[...], g_ref[...]).astype(h_ref.dtype)

    gate = jax.nn.sigmoid(_dot(h_ref[...], wg_ref[...]))
    ple = _dot(p_ref[...].astype(BF16), wp_ref[...])
    o_ref[...] = r_ref[...] + gate * ple


def _ple(x, g, wg, p, wp, tm, tn):
    t, d = x.shape
    pd = p.shape[1]
    tn = _tile(d, tn)
    return pl.pallas_call(
        _ple_kernel,
        grid=(t // tm, d // tn),
        in_specs=[pl.BlockSpec((tm, d), lambda i, j: (i, 0)),
                  pl.BlockSpec((1, d), lambda i, j: (0, 0)),
                  pl.BlockSpec((d, tn), lambda i, j: (0, j)),
                  pl.BlockSpec((tm, pd), lambda i, j: (i, 0)),
                  pl.BlockSpec((pd, tn), lambda i, j: (0, j)),
                  pl.BlockSpec((tm, tn), lambda i, j: (i, j))],
        out_specs=pl.BlockSpec((tm, tn), lambda i, j: (i, j)),
        out_shape=jax.ShapeDtypeStruct((t, d), F32),
        scratch_shapes=[pltpu.VMEM((tm, d), BF16)],
        compiler_params=_params(("parallel", "arbitrary")),
        name="gated_ple",
    )(x, g.reshape(1, d), wg, p, wp, x)


def _layer(x, p, attn_norm_g, w_in, sgu_norm_g, w_s, b_s, q_norm_g, k_norm_g,
           w_up_a, w_up_b, w_o, ffn_norm_g, w_ffn_in, w_ffn_out,
           ple_norm_g, w_ple_gate, w_ple, batch, seq):
    t, d = x.shape
    hidden = w_ffn_out.shape[0]
    tm = _tile(t, 1024)
    blk = _tile(seq, 256)

    w_in = w_in.astype(BF16)
    c_qk = 2 * WIDTH
    c_v = c_qk + 2 * WIDTH
    c_gate = c_v + WIDTH

    h = _rmsnorm(x, attn_norm_g, _tile(t, 512))
    uv = _proj(h, w_in, 0, 2 * WIDTH, "gelu", None, tm, 1024, "proj_sgu")
    scale = GROUP ** -0.5
    qk_gain = jnp.concatenate([jnp.tile(q_norm_g * scale, N_GROUPS),
                               jnp.tile(k_norm_g, N_GROUPS)]).reshape(1, 2 * WIDTH)
    qk = _proj(h, w_in, c_qk, 2 * WIDTH, "headnorm", qk_gain, tm, 1024, "proj_qk")
    v_att = _proj(h, w_in, c_v, WIDTH, "none", None, tm, 1024, "proj_v")
    gates = _proj(h, w_in, c_gate, 2 * d, "sigmoid", None, tm, 1024, "proj_gates")

    y_a = _sgu(uv, sgu_norm_g.reshape(-1), w_s, b_s, _tile(t, 512))
    y_b = _attention(qk, v_att, batch, seq, blk)

    merged = _merge(y_a, y_b, w_up_a.astype(BF16), w_up_b.astype(BF16), gates, tm, 1024)
    x = _matmul_residual(merged, w_o.astype(BF16), x, tm, 1024, "out_proj")

    act = _ffn_in(x, ffn_norm_g, w_ffn_in.astype(BF16), hidden, tm, 512)
    x = _matmul_residual(act, w_ffn_out.astype(BF16), x, tm, 512, "swiglu_out")

    return _ple(x, ple_norm_g, w_ple_gate.astype(BF16), p, w_ple.astype(BF16), tm, 1024)


def kernel(x, p, attn_norm_g, w_in, sgu_norm_g, w_s, b_s, q_norm_g, k_norm_g, w_up_a, w_up_b, w_o,
           ffn_norm_g, w_ffn_in, w_ffn_out, ple_norm_g, w_ple_gate, w_ple):
    batch, seq, d = x.shape
    xf = x.reshape(batch * seq, d)
    for i in range(p.shape[0]):
        xf = _layer(xf, p[i].reshape(batch * seq, -1), attn_norm_g[i], w_in[i], sgu_norm_g[i], w_s[i],
                    b_s[i], q_norm_g[i], k_norm_g[i], w_up_a[i], w_up_b[i], w_o[i], ffn_norm_g[i],
                    w_ffn_in[i], w_ffn_out[i], ple_norm_g[i], w_ple_gate[i], w_ple[i], batch, seq)
    return xf.reshape(batch, seq, d)
```

```python
import functools

import jax
import jax.numpy as jnp
from jax import lax
from jax.experimental import pallas as pl
from jax.experimental.pallas import tpu as pltpu

F32 = jnp.float32
BF16 = jnp.bfloat16

EPS = 1e-6
GROUP = 128
N_GROUPS = 8
WIDTH = GROUP * N_GROUPS

V7X_VMEM_BYTES = 64 * 1024 * 1024
VMEM_LIMIT = V7X_VMEM_BYTES - 8 * 1024 * 1024

EXP_UNDERFLOW = -104.0


def _tile(n, want):
    if n <= want:
        return n
    t = (want // 128) * 128
    while t >= 128:
        if n % t == 0:
            return t
        t -= 128
    return n


def _params(sem):
    return pltpu.CompilerParams(dimension_semantics=sem, vmem_limit_bytes=VMEM_LIMIT)


def _dot(a, b):
    return jnp.dot(a, b, preferred_element_type=F32)


def _rms_rows(x, g):
    y = x * lax.rsqrt(jnp.mean(x * x, axis=-1, keepdims=True) + EPS)
    return y * g


def _rmsnorm_kernel(x_ref, g_ref, o_ref):
    o_ref[...] = _rms_rows(x_ref[...], g_ref[...]).astype(o_ref.dtype)


def _rmsnorm(x, g, tm):
    t, d = x.shape
    return pl.pallas_call(
        _rmsnorm_kernel,
        grid=(t // tm,),
        in_specs=[pl.BlockSpec((tm, d), lambda i: (i, 0)),
                  pl.BlockSpec((1, d), lambda i: (0, 0))],
        out_specs=pl.BlockSpec((tm, d), lambda i: (i, 0)),
        out_shape=jax.ShapeDtypeStruct((t, d), BF16),
        compiler_params=_params(("parallel",)),
        name="rmsnorm",
    )(x, g.reshape(1, d))


def _proj_kernel(h_ref, w_ref, gain_ref, o_ref, *, kind):
    acc = _dot(h_ref[...], w_ref[...])
    if kind == "gelu":
        o_ref[...] = jax.nn.gelu(acc).astype(o_ref.dtype)
    elif kind == "sigmoid":
        o_ref[...] = jax.nn.sigmoid(acc).astype(o_ref.dtype)
    elif kind == "none":
        o_ref[...] = acc.astype(o_ref.dtype)
    elif kind == "headnorm":
        gain = gain_ref[...]
        for c in range(acc.shape[1] // GROUP):
            sl = slice(c * GROUP, (c + 1) * GROUP)
            o_ref[:, sl] = _rms_rows(acc[:, sl], gain[:, sl]).astype(o_ref.dtype)
    else:
        raise ValueError(kind)


def _proj(h, w, col0, ncols, kind, gain, tm, tn, name):
    t, d = h.shape
    tn = _tile(ncols, tn)
    off = col0 // tn
    assert col0 % tn == 0 and ncols % tn == 0
    if gain is None:
        gain = jnp.ones((1, ncols), F32)
    return pl.pallas_call(
        functools.partial(_proj_kernel, kind=kind),
        grid=(t // tm, ncols // tn),
        in_specs=[pl.BlockSpec((tm, d), lambda i, j: (i, 0)),
                  pl.BlockSpec((d, tn), lambda i, j: (0, j + off)),
                  pl.BlockSpec((1, tn), lambda i, j: (0, j))],
        out_specs=pl.BlockSpec((tm, tn), lambda i, j: (i, j)),
        out_shape=jax.ShapeDtypeStruct((t, ncols), BF16),
        compiler_params=_params(("parallel", "parallel")),
        name=name,
    )(h, w, gain)


def _sgu_kernel(u_ref, v_ref, g_ref, ws_ref, b_ref, o_ref, *, tm):
    nchunk = tm // GROUP
    row = lax.broadcasted_iota(jnp.int32, (GROUP, GROUP), 0)
    col = lax.broadcasted_iota(jnp.int32, (GROUP, GROUP), 1)
    causal = row >= col
    for g in range(N_GROUPS):
        sl = slice(g * GROUP, (g + 1) * GROUP)
        vn = _rms_rows(v_ref[:, sl].astype(F32), g_ref[:, sl]).astype(BF16)
        ws = jnp.where(causal, ws_ref[g], 0.0).astype(BF16)
        vcat = jnp.concatenate([vn[c * GROUP:(c + 1) * GROUP, :] for c in range(nchunk)], axis=1)
        mixed = _dot(ws, vcat) + b_ref[:, g:g + 1]
        for c in range(nchunk):
            rows = slice(c * GROUP, (c + 1) * GROUP)
            u = u_ref[rows, sl].astype(F32)
            o_ref[rows, sl] = (u * mixed[:, c * GROUP:(c + 1) * GROUP]).astype(o_ref.dtype)


def _sgu(uv, sgu_g, w_s, b_s, tm):
    t = uv.shape[0]
    return pl.pallas_call(
        functools.partial(_sgu_kernel, tm=tm),
        grid=(t // tm,),
        in_specs=[pl.BlockSpec((tm, WIDTH), lambda i: (i, 0)),
                  pl.BlockSpec((tm, WIDTH), lambda i: (i, 1)),
                  pl.BlockSpec((1, WIDTH), lambda i: (0, 0)),
                  pl.BlockSpec((N_GROUPS, GROUP, GROUP), lambda i: (0, 0, 0)),
                  pl.BlockSpec((GROUP, N_GROUPS), lambda i: (0, 0))],
        out_specs=pl.BlockSpec((tm, WIDTH), lambda i: (i, 0)),
        out_shape=jax.ShapeDtypeStruct((t, WIDTH), BF16),
        compiler_params=_params(("parallel",)),
        name="sgu_mix",
    )(uv, uv, sgu_g.reshape(1, WIDTH), w_s, jnp.transpose(b_s))


def _attn_kernel(q_ref, k_ref, v_ref, o_ref, *, blk, nsub):
    i0 = pl.program_id(2) * nsub
    row = lax.broadcasted_iota(jnp.int32, (blk, blk), 0)
    col = lax.broadcasted_iota(jnp.int32, (blk, blk), 1)
    strict = col < row
    later = strict.astype(BF16)

    def tile(q, j, carry, diagonal):
        start = pl.multiple_of(j * blk, blk)
        k = k_ref[pl.ds(start, blk), :]
        v = v_ref[pl.ds(start, blk), :]
        z = lax.dot_general(q, k, (((1,), (1,)), ((), ())), preferred_element_type=F32)
        log_beta = jnp.minimum(z, 0.0) - jnp.log(1.0 + jnp.exp(-jnp.abs(z)))
        log_1m = log_beta - z
        if diagonal:
            log_1m = jnp.where(strict, log_1m, 0.0)
        hi = log_1m.astype(BF16)
        lo = (log_1m - hi.astype(F32)).astype(BF16)
        tail = _dot(hi, later) + _dot(lo, later) + carry
        w = jnp.exp(log_beta + tail)
        if diagonal:
            w = jnp.where(strict, w, 0.0)
        return _dot(w.astype(BF16), v), jnp.sum(log_1m, axis=1, keepdims=True)

    def step(n, carries, accs, diagonal):
        new_c, new_a = [], []
        for r in range(nsub):
            q = q_ref[r * blk:(r + 1) * blk, :]
            j = i0 + r - n
            if diagonal:
                out, rowsum = tile(q, j, carries[r], True)
            else:
                valid = j >= 0
                out, rowsum = tile(q, jnp.maximum(j, 0),
                                   carries[r] + jnp.where(valid, 0.0, -1e30), False)
                rowsum = jnp.where(valid, rowsum, 0.0)
            new_a.append(accs[r] + out)
            new_c.append(carries[r] + rowsum)
        return tuple(new_c), tuple(new_a)

    carries = tuple(jnp.zeros((blk, 1), F32) for _ in range(nsub))
    accs = tuple(jnp.zeros((blk, GROUP), F32) for _ in range(nsub))
    carries, accs = step(0, carries, accs, True)
    carries, accs = step(1, carries, accs, False)

    def cond(state):
        n, carries, _ = state
        worst = jnp.full((blk, 1), -1e30, F32)
        for r in range(nsub):
            worst = jnp.maximum(worst, jnp.where(i0 + r - n >= 0, carries[r], -1e30))
        return jnp.max(worst) > EXP_UNDERFLOW

    def body(state):
        n, carries, accs = state
        carries, accs = step(n, carries, accs, False)
        return n + 1, carries, accs

    _, _, accs = lax.while_loop(cond, body, (jnp.int32(2), carries, accs))
    for r in range(nsub):
        o_ref[r * blk:(r + 1) * blk, :] = accs[r].astype(o_ref.dtype)


def _attention(qk, v, batch, seq, blk, nsub):
    t = qk.shape[0]
    rows = blk * nsub
    nq = seq // rows
    return pl.pallas_call(
        functools.partial(_attn_kernel, blk=blk, nsub=nsub),
        grid=(batch, N_GROUPS, nq),
        in_specs=[pl.BlockSpec((rows, GROUP), lambda b, h, i: (b * nq + i, h)),
                  pl.BlockSpec((seq, GROUP), lambda b, h, i: (b, N_GROUPS + h)),
                  pl.BlockSpec((seq, GROUP), lambda b, h, i: (b, h))],
        out_specs=pl.BlockSpec((rows, GROUP), lambda b, h, i: (b * nq + i, h)),
        out_shape=jax.ShapeDtypeStruct((t, WIDTH), BF16),
        compiler_params=_params(("parallel", "parallel", "arbitrary")),
        name="stick_breaking_attention",
    )(qk, qk, v)


def _merge_kernel(ya_ref, yb_ref, wa_ref, wb_ref, ga_ref, gb_ref, o_ref):
    a = _dot(ya_ref[...], wa_ref[...])
    b = _dot(yb_ref[...], wb_ref[...])
    o_ref[...] = (ga_ref[...].astype(F32) * a + gb_ref[...].astype(F32) * b).astype(o_ref.dtype)


def _merge(ya, yb, wa, wb, gates, tm):
    t = ya.shape[0]
    d = wa.shape[1]
    return pl.pallas_call(
        _merge_kernel,
        grid=(t // tm,),
        in_specs=[pl.BlockSpec((tm, WIDTH), lambda i: (i, 0)),
                  pl.BlockSpec((tm, WIDTH), lambda i: (i, 0)),
                  pl.BlockSpec((WIDTH, d), lambda i: (0, 0)),
                  pl.BlockSpec((WIDTH, d), lambda i: (0, 0)),
                  pl.BlockSpec((tm, d), lambda i: (i, 0)),
                  pl.BlockSpec((tm, d), lambda i: (i, 1))],
        out_specs=pl.BlockSpec((tm, d), lambda i: (i, 0)),
        out_shape=jax.ShapeDtypeStruct((t, d), BF16),
        compiler_params=_params(("parallel",)),
        name="gated_merge",
    )(ya, yb, wa, wb, gates, gates)


def _out_proj_kernel(a_ref, w_ref, r_ref, g_ref, x_ref, h_ref):
    x = r_ref[...] + _dot(a_ref[...], w_ref[...])
    x_ref[...] = x
    h_ref[...] = _rms_rows(x, g_ref[...]).astype(h_ref.dtype)


def _out_proj(a, w, r, g, tm):
    t, k = a.shape
    d = w.shape[1]
    return pl.pallas_call(
        _out_proj_kernel,
        grid=(t // tm,),
        in_specs=[pl.BlockSpec((tm, k), lambda i: (i, 0)),
                  pl.BlockSpec((k, d), lambda i: (0, 0)),
                  pl.BlockSpec((tm, d), lambda i: (i, 0)),
                  pl.BlockSpec((1, d), lambda i: (0, 0))],
        out_specs=[pl.BlockSpec((tm, d), lambda i: (i, 0)),
                   pl.BlockSpec((tm, d), lambda i: (i, 0))],
        out_shape=[jax.ShapeDtypeStruct((t, d), F32), jax.ShapeDtypeStruct((t, d), BF16)],
        compiler_params=_params(("parallel",)),
        name="out_proj",
    )(a, w, r, g.reshape(1, d))


def _ffn_in_kernel(h_ref, wg_ref, wu_ref, o_ref):
    h = h_ref[...]
    gate = _dot(h, wg_ref[...])
    up = _dot(h, wu_ref[...])
    o_ref[...] = (jax.nn.silu(gate) * up).astype(o_ref.dtype)


def _ffn_in(h, w, hidden, tm, tn):
    t, d = h.shape
    tn = _tile(hidden, tn)
    nb = hidden // tn
    return pl.pallas_call(
        _ffn_in_kernel,
        grid=(t // tm, nb),
        in_specs=[pl.BlockSpec((tm, d), lambda i, j: (i, 0)),
                  pl.BlockSpec((d, tn), lambda i, j: (0, j)),
                  pl.BlockSpec((d, tn), lambda i, j: (0, j + nb))],
        out_specs=pl.BlockSpec((tm, tn), lambda i, j: (i, j)),
        out_shape=jax.ShapeDtypeStruct((t, hidden), BF16),
        compiler_params=_params(("parallel", "parallel")),
        name="swiglu_in",
    )(h, w, w)


def _matmul_residual_kernel(a_ref, w_ref, r_ref, o_ref):
    o_ref[...] = r_ref[...] + _dot(a_ref[...], w_ref[...])


def _matmul_residual(a, w, r, tm, tn, name):
    t, k = a.shape
    d = w.shape[1]
    tn = _tile(d, tn)
    return pl.pallas_call(
        _matmul_residual_kernel,
        grid=(t // tm, d // tn),
        in_specs=[pl.BlockSpec((tm, k), lambda i, j: (i, 0)),
                  pl.BlockSpec((k, tn), lambda i, j: (0, j)),
                  pl.BlockSpec((tm, tn), lambda i, j: (i, j))],
        out_specs=pl.BlockSpec((tm, tn), lambda i, j: (i, j)),
        out_shape=jax.ShapeDtypeStruct((t, d), F32),
        compiler_params=_params(("parallel", "parallel")),
        name=name,
    )(a, w, r)


def _ple_kernel(x_ref, g_ref, wg_ref, p_ref, wp_ref, o_ref):
    x = x_ref[...]
    h = _rms_rows(x, g_ref[...]).astype(BF16)
    gate = jax.nn.sigmoid(_dot(h, wg_ref[...]))
    ple = _dot(p_ref[...].astype(BF16), wp_ref[...])
    o_ref[...] = x + gate * ple


def _ple(x, g, wg, p, wp, tm):
    t, d = x.shape
    pd = p.shape[1]
    return pl.pallas_call(
        _ple_kernel,
        grid=(t // tm,),
        in_specs=[pl.BlockSpec((tm, d), lambda i: (i, 0)),
                  pl.BlockSpec((1, d), lambda i: (0, 0)),
                  pl.BlockSpec((d, d), lambda i: (0, 0)),
                  pl.BlockSpec((tm, pd), lambda i: (i, 0)),
                  pl.BlockSpec((pd, d), lambda i: (0, 0))],
        out_specs=pl.BlockSpec((tm, d), lambda i: (i, 0)),
        out_shape=jax.ShapeDtypeStruct((t, d), F32),
        compiler_params=_params(("parallel",)),
        name="gated_ple",
    )(x, g.reshape(1, d), wg, p, wp)


def _layer(x, p, attn_norm_g, w_in, sgu_norm_g, w_s, b_s, q_norm_g, k_norm_g,
           w_up_a, w_up_b, w_o, ffn_norm_g, w_ffn_in, w_ffn_out,
           ple_norm_g, w_ple_gate, w_ple, batch, seq):
    t, d = x.shape
    hidden = w_ffn_out.shape[0]
    tm = _tile(t, 1024)
    tm_full = _tile(t, 512)
    blk = _tile(seq, 256)
    nsub = 4 if seq % (4 * blk) == 0 else 1

    w_in = w_in.astype(BF16)
    c_qk = 2 * WIDTH
    c_v = c_qk + 2 * WIDTH
    c_gate = c_v + WIDTH

    h = _rmsnorm(x, attn_norm_g, tm_full)
    uv = _proj(h, w_in, 0, 2 * WIDTH, "gelu", None, tm, 1024, "proj_sgu")
    scale = GROUP ** -0.5
    qk_gain = jnp.concatenate([jnp.tile(q_norm_g * scale, N_GROUPS),
                               jnp.tile(k_norm_g, N_GROUPS)]).reshape(1, 2 * WIDTH)
    qk = _proj(h, w_in, c_qk, 2 * WIDTH, "headnorm", qk_gain, tm, 1024, "proj_qk")
    v_att = _proj(h, w_in, c_v, WIDTH, "none", None, tm, 1024, "proj_v")
    gates = _proj(h, w_in, c_gate, 2 * d, "sigmoid", None, tm, 1024, "proj_gates")

    y_a = _sgu(uv, sgu_norm_g.reshape(-1), w_s, b_s, tm_full)
    y_b = _attention(qk, v_att, batch, seq, blk, nsub)

    merged = _merge(y_a, y_b, w_up_a.astype(BF16), w_up_b.astype(BF16), gates, tm_full)
    x, h = _out_proj(merged, w_o.astype(BF16), x, ffn_norm_g, tm_full)

    act = _ffn_in(h, w_ffn_in.astype(BF16), hidden, tm, 512)
    x = _matmul_residual(act, w_ffn_out.astype(BF16), x, tm, 512, "swiglu_out")

    return _ple(x, ple_norm_g, w_ple_gate.astype(BF16), p, w_ple.astype(BF16), tm_full)


def kernel(x, p, attn_norm_g, w_in, sgu_norm_g, w_s, b_s, q_norm_g, k_norm_g, w_up_a, w_up_b, w_o,
           ffn_norm_g, w_ffn_in, w_ffn_out, ple_norm_g, w_ple_gate, w_ple):
    batch, seq, d = x.shape
    xf = x.reshape(batch * seq, d)
    for i in range(p.shape[0]):
        xf = _layer(xf, p[i].reshape(batch * seq, -1), attn_norm_g[i], w_in[i], sgu_norm_g[i], w_s[i],
                    b_s[i], q_norm_g[i], k_norm_g[i], w_up_a[i], w_up_b[i], w_o[i], ffn_norm_g[i],
                    w_ffn_in[i], w_ffn_out[i], ple_norm_g[i], w_ple_gate[i], w_ple[i], batch, seq)
    return xf.reshape(batch, seq, d)
```

```python
import functools

import jax
import jax.numpy as jnp
from jax import lax
from jax.experimental import pallas as pl
from jax.experimental.pallas import tpu as pltpu

F32 = jnp.float32
BF16 = jnp.bfloat16

EPS = 1e-6
GROUP = 128
N_GROUPS = 8
WIDTH = GROUP * N_GROUPS

V7X_VMEM_BYTES = 64 * 1024 * 1024
VMEM_LIMIT = V7X_VMEM_BYTES - 8 * 1024 * 1024

EXP_UNDERFLOW = -104.0


def _tile(n, want):
    if n <= want:
        return n
    t = (want // 128) * 128
    while t >= 128:
        if n % t == 0:
            return t
        t -= 128
    return n


def _params(sem):
    return pltpu.CompilerParams(dimension_semantics=sem, vmem_limit_bytes=VMEM_LIMIT)


def _dot(a, b):
    return jnp.dot(a, b, preferred_element_type=F32)


def _sigmoid(x):
    return 0.5 * jnp.tanh(0.5 * x) + 0.5


def _rms_rows(x, g):
    y = x * lax.rsqrt(jnp.mean(x * x, axis=-1, keepdims=True) + EPS)
    return y * g


def _proj_sgu_kernel(x_ref, g_ref, w_ref, o_ref, h_ref):
    h = _rms_rows(x_ref[...], g_ref[...]).astype(h_ref.dtype)
    h_ref[...] = h
    o_ref[...] = jax.nn.gelu(_dot(h, w_ref[...])).astype(o_ref.dtype)


def _proj_sgu(x, g, w, tm):
    t, d = x.shape
    ncols = w.shape[1]
    return pl.pallas_call(
        _proj_sgu_kernel,
        grid=(t // tm,),
        in_specs=[pl.BlockSpec((tm, d), lambda i: (i, 0)),
                  pl.BlockSpec((1, d), lambda i: (0, 0)),
                  pl.BlockSpec((d, ncols), lambda i: (0, 0))],
        out_specs=[pl.BlockSpec((tm, ncols), lambda i: (i, 0)),
                   pl.BlockSpec((tm, d), lambda i: (i, 0))],
        out_shape=[jax.ShapeDtypeStruct((t, ncols), BF16), jax.ShapeDtypeStruct((t, d), BF16)],
        compiler_params=_params(("parallel",)),
        name="proj_sgu",
    )(x, g.reshape(1, d), w)


def _proj_kernel(h_ref, w_ref, gain_ref, o_ref, *, kind):
    acc = _dot(h_ref[...], w_ref[...])
    if kind == "sigmoid":
        o_ref[...] = _sigmoid(acc).astype(o_ref.dtype)
    elif kind == "none":
        o_ref[...] = acc.astype(o_ref.dtype)
    elif kind == "headnorm":
        gain = gain_ref[...]
        for c in range(acc.shape[1] // GROUP):
            sl = slice(c * GROUP, (c + 1) * GROUP)
            o_ref[:, sl] = _rms_rows(acc[:, sl], gain[:, sl]).astype(o_ref.dtype)
    else:
        raise ValueError(kind)


def _proj(h, w, kind, gain, tm, tn, name):
    t, d = h.shape
    ncols = w.shape[1]
    tn = _tile(ncols, tn)
    if gain is None:
        gain = jnp.ones((1, ncols), F32)
    return pl.pallas_call(
        functools.partial(_proj_kernel, kind=kind),
        grid=(t // tm, ncols // tn),
        in_specs=[pl.BlockSpec((tm, d), lambda i, j: (i, 0)),
                  pl.BlockSpec((d, tn), lambda i, j: (0, j)),
                  pl.BlockSpec((1, tn), lambda i, j: (0, j))],
        out_specs=pl.BlockSpec((tm, tn), lambda i, j: (i, j)),
        out_shape=jax.ShapeDtypeStruct((t, ncols), BF16),
        compiler_params=_params(("parallel", "parallel")),
        name=name,
    )(h, w, gain)


def _attn_kernel(q_ref, k_ref, v_ref, o_ref, *, blk, nsub):
    i0 = pl.program_id(2) * nsub
    row = lax.broadcasted_iota(jnp.int32, (blk, blk), 0)
    col = lax.broadcasted_iota(jnp.int32, (blk, blk), 1)
    strict = col < row
    later = strict.astype(BF16)

    def tile(q, j, carry, diagonal):
        start = pl.multiple_of(j * blk, blk)
        k = k_ref[pl.ds(start, blk), :]
        v = v_ref[pl.ds(start, blk), :]
        z = lax.dot_general(q, k, (((1,), (1,)), ((), ())), preferred_element_type=F32)
        log_beta = jnp.minimum(z, 0.0) - jnp.log(1.0 + jnp.exp(-jnp.abs(z)))
        log_1m = log_beta - z
        if diagonal:
            log_1m = jnp.where(strict, log_1m, 0.0)
        tail = _dot(log_1m.astype(BF16), later) + carry
        w = jnp.exp(log_beta + tail)
        if diagonal:
            w = jnp.where(strict, w, 0.0)
        return _dot(w.astype(BF16), v), jnp.sum(log_1m, axis=1, keepdims=True)

    def step(n, carries, accs, diagonal):
        new_c, new_a = [], []
        for r in range(nsub):
            q = q_ref[r * blk:(r + 1) * blk, :]
            j = i0 + r - n
            if diagonal:
                out, rowsum = tile(q, j, carries[r], True)
            else:
                valid = j >= 0
                out, rowsum = tile(q, jnp.maximum(j, 0),
                                   carries[r] + jnp.where(valid, 0.0, -1e30), False)
                rowsum = jnp.where(valid, rowsum, 0.0)
            new_a.append(accs[r] + out)
            new_c.append(carries[r] + rowsum)
        return tuple(new_c), tuple(new_a)

    carries = tuple(jnp.zeros((blk, 1), F32) for _ in range(nsub))
    accs = tuple(jnp.zeros((blk, GROUP), F32) for _ in range(nsub))
    carries, accs = step(0, carries, accs, True)
    carries, accs = step(1, carries, accs, False)

    def cond(state):
        n, carries, _ = state
        worst = jnp.full((blk, 1), -1e30, F32)
        for r in range(nsub):
            worst = jnp.maximum(worst, jnp.where(i0 + r - n >= 0, carries[r], -1e30))
        return jnp.max(worst) > EXP_UNDERFLOW

    def body(state):
        n, carries, accs = state
        carries, accs = step(n, carries, accs, False)
        return n + 1, carries, accs

    _, _, accs = lax.while_loop(cond, body, (jnp.int32(2), carries, accs))
    for r in range(nsub):
        o_ref[r * blk:(r + 1) * blk, :] = accs[r].astype(o_ref.dtype)


def _attention(qk, v, batch, seq, blk, nsub):
    t = qk.shape[0]
    rows = blk * nsub
    nq = seq // rows
    return pl.pallas_call(
        functools.partial(_attn_kernel, blk=blk, nsub=nsub),
        grid=(batch, N_GROUPS, nq),
        in_specs=[pl.BlockSpec((rows, GROUP), lambda b, h, i: (b * nq + i, h)),
                  pl.BlockSpec((seq, GROUP), lambda b, h, i: (b, N_GROUPS + h)),
                  pl.BlockSpec((seq, GROUP), lambda b, h, i: (b, h))],
        out_specs=pl.BlockSpec((rows, GROUP), lambda b, h, i: (b * nq + i, h)),
        out_shape=jax.ShapeDtypeStruct((t, WIDTH), BF16),
        compiler_params=_params(("parallel", "parallel", "arbitrary")),
        name="stick_breaking_attention",
    )(qk, qk, v)


def _sgu_tile(u_ref, v_ref, g_ref, ws_ref, b_ref, y_ref, tm):
    nchunk = tm // GROUP
    row = lax.broadcasted_iota(jnp.int32, (GROUP, GROUP), 0)
    col = lax.broadcasted_iota(jnp.int32, (GROUP, GROUP), 1)
    causal = row >= col
    for g in range(N_GROUPS):
        sl = slice(g * GROUP, (g + 1) * GROUP)
        vn = _rms_rows(v_ref[:, sl].astype(F32), g_ref[:, sl]).astype(BF16)
        ws = jnp.where(causal, ws_ref[g], 0.0).astype(BF16)
        vcat = jnp.concatenate([vn[c * GROUP:(c + 1) * GROUP, :] for c in range(nchunk)], axis=1)
        mixed = _dot(ws, vcat) + b_ref[:, g:g + 1]
        for c in range(nchunk):
            rows = slice(c * GROUP, (c + 1) * GROUP)
            u = u_ref[rows, sl].astype(F32)
            y_ref[rows, sl] = (u * mixed[:, c * GROUP:(c + 1) * GROUP]).astype(y_ref.dtype)


def _merge_kernel(u_ref, v_ref, sg_ref, ws_ref, b_ref, yb_ref, wa_ref, wb_ref, ga_ref, gb_ref, o_ref,
                  ya_ref, *, tm):
    _sgu_tile(u_ref, v_ref, sg_ref, ws_ref, b_ref, ya_ref, tm)
    b = _dot(yb_ref[...], wb_ref[...])
    a = _dot(ya_ref[...], wa_ref[...])
    o_ref[...] = (ga_ref[...].astype(F32) * a + gb_ref[...].astype(F32) * b).astype(o_ref.dtype)


def _merge(uv, sgu_g, w_s, b_s, yb, wa, wb, gates, tm):
    t = yb.shape[0]
    d = wa.shape[1]
    return pl.pallas_call(
        functools.partial(_merge_kernel, tm=tm),
        grid=(t // tm,),
        in_specs=[pl.BlockSpec((tm, WIDTH), lambda i: (i, 0)),
                  pl.BlockSpec((tm, WIDTH), lambda i: (i, 1)),
                  pl.BlockSpec((1, WIDTH), lambda i: (0, 0)),
                  pl.BlockSpec((N_GROUPS, GROUP, GROUP), lambda i: (0, 0, 0)),
                  pl.BlockSpec((GROUP, N_GROUPS), lambda i: (0, 0)),
                  pl.BlockSpec((tm, WIDTH), lambda i: (i, 0)),
                  pl.BlockSpec((WIDTH, d), lambda i: (0, 0)),
                  pl.BlockSpec((WIDTH, d), lambda i: (0, 0)),
                  pl.BlockSpec((tm, d), lambda i: (i, 0)),
                  pl.BlockSpec((tm, d), lambda i: (i, 1))],
        out_specs=pl.BlockSpec((tm, d), lambda i: (i, 0)),
        out_shape=jax.ShapeDtypeStruct((t, d), BF16),
        scratch_shapes=[pltpu.VMEM((tm, WIDTH), BF16)],
        compiler_params=_params(("parallel",)),
        name="sgu_gated_merge",
    )(uv, uv, sgu_g.reshape(1, WIDTH), w_s, jnp.transpose(b_s), yb, wa, wb, gates, gates)


def _out_proj_kernel(a_ref, w_ref, r_ref, g_ref, x_ref, h_ref):
    x = r_ref[...] + _dot(a_ref[...], w_ref[...])
    x_ref[...] = x
    h_ref[...] = _rms_rows(x, g_ref[...]).astype(h_ref.dtype)


def _out_proj(a, w, r, g, tm):
    t, k = a.shape
    d = w.shape[1]
    return pl.pallas_call(
        _out_proj_kernel,
        grid=(t // tm,),
        in_specs=[pl.BlockSpec((tm, k), lambda i: (i, 0)),
                  pl.BlockSpec((k, d), lambda i: (0, 0)),
                  pl.BlockSpec((tm, d), lambda i: (i, 0)),
                  pl.BlockSpec((1, d), lambda i: (0, 0))],
        out_specs=[pl.BlockSpec((tm, d), lambda i: (i, 0)),
                   pl.BlockSpec((tm, d), lambda i: (i, 0))],
        out_shape=[jax.ShapeDtypeStruct((t, d), F32), jax.ShapeDtypeStruct((t, d), BF16)],
        compiler_params=_params(("parallel",)),
        name="out_proj",
    )(a, w, r, g.reshape(1, d))


def _ffn_in_kernel(h_ref, wg_ref, wu_ref, o_ref):
    h = h_ref[...]
    gate = _dot(h, wg_ref[...])
    up = _dot(h, wu_ref[...])
    o_ref[...] = (gate * _sigmoid(gate) * up).astype(o_ref.dtype)


def _ffn_in(h, w, hidden, tm, tn):
    t, d = h.shape
    tn = _tile(hidden, tn)
    nb = hidden // tn
    return pl.pallas_call(
        _ffn_in_kernel,
        grid=(t // tm, nb),
        in_specs=[pl.BlockSpec((tm, d), lambda i, j: (i, 0)),
                  pl.BlockSpec((d, tn), lambda i, j: (0, j)),
                  pl.BlockSpec((d, tn), lambda i, j: (0, j + nb))],
        out_specs=pl.BlockSpec((tm, tn), lambda i, j: (i, j)),
        out_shape=jax.ShapeDtypeStruct((t, hidden), BF16),
        compiler_params=_params(("parallel", "parallel")),
        name="swiglu_in",
    )(h, w, w)


def _matmul_residual_kernel(a_ref, w_ref, r_ref, o_ref):
    o_ref[...] = r_ref[...] + _dot(a_ref[...], w_ref[...])


def _matmul_residual(a, w, r, tm, tn, name):
    t, k = a.shape
    d = w.shape[1]
    tn = _tile(d, tn)
    return pl.pallas_call(
        _matmul_residual_kernel,
        grid=(t // tm, d // tn),
        in_specs=[pl.BlockSpec((tm, k), lambda i, j: (i, 0)),
                  pl.BlockSpec((k, tn), lambda i, j: (0, j)),
                  pl.BlockSpec((tm, tn), lambda i, j: (i, j))],
        out_specs=pl.BlockSpec((tm, tn), lambda i, j: (i, j)),
        out_shape=jax.ShapeDtypeStruct((t, d), F32),
        compiler_params=_params(("parallel", "parallel")),
        name=name,
    )(a, w, r)


def _ple_kernel(x_ref, g_ref, wg_ref, p_ref, wp_ref, o_ref):
    x = x_ref[...]
    h = _rms_rows(x, g_ref[...]).astype(BF16)
    gate = jax.nn.sigmoid(_dot(h, wg_ref[...]))
    ple = _dot(p_ref[...].astype(BF16), wp_ref[...])
    o_ref[...] = x + gate * ple


def _ple(x, g, wg, p, wp, tm):
    t, d = x.shape
    pd = p.shape[1]
    return pl.pallas_call(
        _ple_kernel,
        grid=(t // tm,),
        in_specs=[pl.BlockSpec((tm, d), lambda i: (i, 0)),
                  pl.BlockSpec((1, d), lambda i: (0, 0)),
                  pl.BlockSpec((d, d), lambda i: (0, 0)),
                  pl.BlockSpec((tm, pd), lambda i: (i, 0)),
                  pl.BlockSpec((pd, d), lambda i: (0, 0))],
        out_specs=pl.BlockSpec((tm, d), lambda i: (i, 0)),
        out_shape=jax.ShapeDtypeStruct((t, d), F32),
        compiler_params=_params(("parallel",)),
        name="gated_ple",
    )(x, g.reshape(1, d), wg, p, wp)


def _layer(x, p, attn_norm_g, w_in, sgu_norm_g, w_s, b_s, q_norm_g, k_norm_g,
           w_up_a, w_up_b, w_o, ffn_norm_g, w_ffn_in, w_ffn_out,
           ple_norm_g, w_ple_gate, w_ple, batch, seq):
    t, d = x.shape
    hidden = w_ffn_out.shape[0]
    tm = _tile(t, 1024)
    tm_full = _tile(t, 512)
    tm_big = _tile(t, 2048)
    blk = _tile(seq, 256)
    nsub = 4 if seq % (4 * blk) == 0 else 1

    c_qk = 2 * WIDTH
    c_v = c_qk + 2 * WIDTH
    c_gate = c_v + WIDTH
    w_sgu = w_in[:, :c_qk].astype(BF16)
    w_qk = w_in[:, c_qk:c_v].astype(BF16)
    w_v = w_in[:, c_v:c_gate].astype(BF16)
    w_gates = w_in[:, c_gate:].astype(BF16)

    uv, h = _proj_sgu(x, attn_norm_g, w_sgu, tm_full)
    scale = GROUP ** -0.5
    qk_gain = jnp.concatenate([jnp.tile(q_norm_g * scale, N_GROUPS),
                               jnp.tile(k_norm_g, N_GROUPS)]).reshape(1, 2 * WIDTH)
    qk = _proj(h, w_qk, "headnorm", qk_gain, tm, 2048, "proj_qk")
    v_att = _proj(h, w_v, "none", None, tm_big, 1024, "proj_v")
    gates = _proj(h, w_gates, "sigmoid", None, tm, 2048, "proj_gates")

    y_b = _attention(qk, v_att, batch, seq, blk, nsub)

    merged = _merge(uv, sgu_norm_g.reshape(-1), w_s, b_s, y_b,
                    w_up_a.astype(BF16), w_up_b.astype(BF16), gates, tm_full)
    x, h = _out_proj(merged, w_o.astype(BF16), x, ffn_norm_g, tm_full)

    act = _ffn_in(h, w_ffn_in.astype(BF16), hidden, tm, 512)
    x = _matmul_residual(act, w_ffn_out.astype(BF16), x, tm, 512, "swiglu_out")

    return _ple(x, ple_norm_g, w_ple_gate.astype(BF16), p, w_ple.astype(BF16), tm_full)


def kernel(x, p, attn_norm_g, w_in, sgu_norm_g, w_s, b_s, q_norm_g, k_norm_g, w_up_a, w_up_b, w_o,
           ffn_norm_g, w_ffn_in, w_ffn_out, ple_norm_g, w_ple_gate, w_ple):
    batch, seq, d = x.shape
    xf = x.reshape(batch * seq, d)
    for i in range(p.shape[0]):
        xf = _layer(xf, p[i].reshape(batch * seq, -1), attn_norm_g[i], w_in[i], sgu_norm_g[i], w_s[i],
                    b_s[i], q_norm_g[i], k_norm_g[i], w_up_a[i], w_up_b[i], w_o[i], ffn_norm_g[i],
                    w_ffn_in[i], w_ffn_out[i], ple_norm_g[i], w_ple_gate[i], w_ple[i], batch, seq)
    return xf.reshape(batch, seq, d)
```

```python
import functools

import jax
import jax.numpy as jnp
from jax import lax
from jax.experimental import pallas as pl
from jax.experimental.pallas import tpu as pltpu

F32 = jnp.float32
BF16 = jnp.bfloat16

EPS = 1e-6
GROUP = 128
N_GROUPS = 8
WIDTH = GROUP * N_GROUPS

V7X_VMEM_BYTES = 64 * 1024 * 1024
VMEM_LIMIT = V7X_VMEM_BYTES - 8 * 1024 * 1024

EXP_UNDERFLOW = -104.0

ROW_CHUNK = 128


def _tile(n, want):
    if n <= want:
        return n
    t = (want // 128) * 128
    while t >= 128:
        if n % t == 0:
            return t
        t -= 128
    return n


def _params(sem):
    return pltpu.CompilerParams(dimension_semantics=sem, vmem_limit_bytes=VMEM_LIMIT)


def _dot(a, b):
    return jnp.dot(a, b, preferred_element_type=F32)


def _sigmoid(x):
    return 0.5 * jnp.tanh(0.5 * x) + 0.5


def _row_chunks(n):
    step = min(ROW_CHUNK, n)
    return [slice(r, r + step) for r in range(0, n, step)]


def _rms_rows(x, g):
    y = x * lax.rsqrt(jnp.mean(x * x, axis=-1, keepdims=True) + EPS)
    return y * g


def _cast_weight_once(w_ref, wb_ref, axis):
    @pl.when(pl.program_id(axis) == 0)
    def _():
        wb_ref[...] = w_ref[...].astype(wb_ref.dtype)


def _proj_sgu_kernel(x_ref, g_ref, w_ref, o_ref, h_ref, wb_ref):
    _cast_weight_once(w_ref, wb_ref, 0)
    for rows in _row_chunks(x_ref.shape[0]):
        h = _rms_rows(x_ref[rows, :], g_ref[...]).astype(h_ref.dtype)
        h_ref[rows, :] = h
        o_ref[rows, :] = jax.nn.gelu(_dot(h, wb_ref[...])).astype(o_ref.dtype)


def _proj_sgu(x, g, w, ncols, tm):
    t, d = x.shape
    return pl.pallas_call(
        _proj_sgu_kernel,
        grid=(t // tm,),
        in_specs=[pl.BlockSpec((tm, d), lambda i: (i, 0)),
                  pl.BlockSpec((1, d), lambda i: (0, 0)),
                  pl.BlockSpec((d, ncols), lambda i: (0, 0))],
        out_specs=[pl.BlockSpec((tm, ncols), lambda i: (i, 0)),
                   pl.BlockSpec((tm, d), lambda i: (i, 0))],
        out_shape=[jax.ShapeDtypeStruct((t, ncols), BF16), jax.ShapeDtypeStruct((t, d), BF16)],
        scratch_shapes=[pltpu.VMEM((d, ncols), BF16)],
        compiler_params=_params(("arbitrary",)),
        name="proj_sgu",
    )(x, g.reshape(1, d), w)


def _proj_kernel(h_ref, w_ref, gain_ref, o_ref, wb_ref, *, kind):
    _cast_weight_once(w_ref, wb_ref, 1)
    acc = _dot(h_ref[...], wb_ref[...])
    if kind == "sigmoid":
        o_ref[...] = _sigmoid(acc).astype(o_ref.dtype)
    elif kind == "none":
        o_ref[...] = acc.astype(o_ref.dtype)
    elif kind == "headnorm":
        gain = gain_ref[...]
        for c in range(acc.shape[1] // GROUP):
            sl = slice(c * GROUP, (c + 1) * GROUP)
            o_ref[:, sl] = _rms_rows(acc[:, sl], gain[:, sl]).astype(o_ref.dtype)
    else:
        raise ValueError(kind)


def _proj(h, w, col0, ncols, kind, gain, tm, tn, name):
    t, d = h.shape
    tn = _tile(ncols, tn)
    assert col0 % tn == 0
    off = col0 // tn
    if gain is None:
        gain = jnp.ones((1, ncols), F32)
    return pl.pallas_call(
        functools.partial(_proj_kernel, kind=kind),
        grid=(ncols // tn, t // tm),
        in_specs=[pl.BlockSpec((tm, d), lambda j, i: (i, 0)),
                  pl.BlockSpec((d, tn), lambda j, i: (0, j + off)),
                  pl.BlockSpec((1, tn), lambda j, i: (0, j))],
        out_specs=pl.BlockSpec((tm, tn), lambda j, i: (i, j)),
        out_shape=jax.ShapeDtypeStruct((t, ncols), BF16),
        scratch_shapes=[pltpu.VMEM((d, tn), BF16)],
        compiler_params=_params(("parallel", "arbitrary")),
        name=name,
    )(h, w, gain)


def _attn_kernel(q_ref, k_ref, v_ref, o_ref, *, blk, nsub):
    i0 = pl.program_id(2) * nsub
    row = lax.broadcasted_iota(jnp.int32, (blk, blk), 0)
    col = lax.broadcasted_iota(jnp.int32, (blk, blk), 1)
    strict = col < row
    later = strict.astype(BF16)

    def tile(q, j, carry, diagonal):
        start = pl.multiple_of(j * blk, blk)
        k = k_ref[pl.ds(start, blk), :]
        v = v_ref[pl.ds(start, blk), :]
        z = lax.dot_general(q, k, (((1,), (1,)), ((), ())), preferred_element_type=F32)
        log_beta = jnp.minimum(z, 0.0) - jnp.log(1.0 + jnp.exp(-jnp.abs(z)))
        log_1m = log_beta - z
        if diagonal:
            log_1m = jnp.where(strict, log_1m, 0.0)
        tail = _dot(log_1m.astype(BF16), later) + carry
        w = jnp.exp(log_beta + tail)
        if diagonal:
            w = jnp.where(strict, w, 0.0)
        return _dot(w.astype(BF16), v), jnp.sum(log_1m, axis=1, keepdims=True)

    def step(n, carries, accs, diagonal):
        new_c, new_a = [], []
        for r in range(nsub):
            q = q_ref[r * blk:(r + 1) * blk, :]
            j = i0 + r - n
            if diagonal:
                out, rowsum = tile(q, j, carries[r], True)
            else:
                valid = j >= 0
                out, rowsum = tile(q, jnp.maximum(j, 0),
                                   carries[r] + jnp.where(valid, 0.0, -1e30), False)
                rowsum = jnp.where(valid, rowsum, 0.0)
            new_a.append(accs[r] + out)
            new_c.append(carries[r] + rowsum)
        return tuple(new_c), tuple(new_a)

    carries = tuple(jnp.zeros((blk, 1), F32) for _ in range(nsub))
    accs = tuple(jnp.zeros((blk, GROUP), F32) for _ in range(nsub))
    carries, accs = step(0, carries, accs, True)
    carries, accs = step(1, carries, accs, False)

    def cond(state):
        n, carries, _ = state
        worst = jnp.full((blk, 1), -1e30, F32)
        for r in range(nsub):
            worst = jnp.maximum(worst, jnp.where(i0 + r - n >= 0, carries[r], -1e30))
        return jnp.max(worst) > EXP_UNDERFLOW

    def body(state):
        n, carries, accs = state
        carries, accs = step(n, carries, accs, False)
        return n + 1, carries, accs

    _, _, accs = lax.while_loop(cond, body, (jnp.int32(2), carries, accs))
    for r in range(nsub):
        o_ref[r * blk:(r + 1) * blk, :] = accs[r].astype(o_ref.dtype)


def _attention(qk, v, batch, seq, blk, nsub):
    t = qk.shape[0]
    rows = blk * nsub
    nq = seq // rows
    return pl.pallas_call(
        functools.partial(_attn_kernel, blk=blk, nsub=nsub),
        grid=(batch, N_GROUPS, nq),
        in_specs=[pl.BlockSpec((rows, GROUP), lambda b, h, i: (b * nq + i, h)),
                  pl.BlockSpec((seq, GROUP), lambda b, h, i: (b, N_GROUPS + h)),
                  pl.BlockSpec((seq, GROUP), lambda b, h, i: (b, h))],
        out_specs=pl.BlockSpec((rows, GROUP), lambda b, h, i: (b * nq + i, h)),
        out_shape=jax.ShapeDtypeStruct((t, WIDTH), BF16),
        compiler_params=_params(("parallel", "parallel", "arbitrary")),
        name="stick_breaking_attention",
    )(qk, qk, v)


def _sgu_tile(u_ref, v_ref, g_ref, ws_ref, b_ref, y_ref, tm):
    nchunk = tm // GROUP
    row = lax.broadcasted_iota(jnp.int32, (GROUP, GROUP), 0)
    col = lax.broadcasted_iota(jnp.int32, (GROUP, GROUP), 1)
    causal = row >= col
    for g in range(N_GROUPS):
        sl = slice(g * GROUP, (g + 1) * GROUP)
        vn = _rms_rows(v_ref[:, sl].astype(F32), g_ref[:, sl]).astype(BF16)
        ws = jnp.where(causal, ws_ref[g], 0.0).astype(BF16)
        vcat = jnp.concatenate([vn[c * GROUP:(c + 1) * GROUP, :] for c in range(nchunk)], axis=1)
        mixed = _dot(ws, vcat) + b_ref[:, g:g + 1]
        for c in range(nchunk):
            rows = slice(c * GROUP, (c + 1) * GROUP)
            u = u_ref[rows, sl].astype(F32)
            y_ref[rows, sl] = (u * mixed[:, c * GROUP:(c + 1) * GROUP]).astype(y_ref.dtype)


def _merge_kernel(u_ref, v_ref, sg_ref, ws_ref, b_ref, yb_ref, wa_ref, wb_ref, ga_ref, gb_ref, o_ref,
                  ya_ref, *, tm):
    _sgu_tile(u_ref, v_ref, sg_ref, ws_ref, b_ref, ya_ref, tm)
    b = _dot(yb_ref[...], wb_ref[...])
    a = _dot(ya_ref[...], wa_ref[...])
    o_ref[...] = (ga_ref[...].astype(F32) * a + gb_ref[...].astype(F32) * b).astype(o_ref.dtype)


def _merge(uv, sgu_g, w_s, b_s, yb, wa, wb, gates, tm):
    t = yb.shape[0]
    d = wa.shape[1]
    return pl.pallas_call(
        functools.partial(_merge_kernel, tm=tm),
        grid=(t // tm,),
        in_specs=[pl.BlockSpec((tm, WIDTH), lambda i: (i, 0)),
                  pl.BlockSpec((tm, WIDTH), lambda i: (i, 1)),
                  pl.BlockSpec((1, WIDTH), lambda i: (0, 0)),
                  pl.BlockSpec((N_GROUPS, GROUP, GROUP), lambda i: (0, 0, 0)),
                  pl.BlockSpec((GROUP, N_GROUPS), lambda i: (0, 0)),
                  pl.BlockSpec((tm, WIDTH), lambda i: (i, 0)),
                  pl.BlockSpec((WIDTH, d), lambda i: (0, 0)),
                  pl.BlockSpec((WIDTH, d), lambda i: (0, 0)),
                  pl.BlockSpec((tm, d), lambda i: (i, 0)),
                  pl.BlockSpec((tm, d), lambda i: (i, 1))],
        out_specs=pl.BlockSpec((tm, d), lambda i: (i, 0)),
        out_shape=jax.ShapeDtypeStruct((t, d), BF16),
        scratch_shapes=[pltpu.VMEM((tm, WIDTH), BF16)],
        compiler_params=_params(("parallel",)),
        name="sgu_gated_merge",
    )(uv, uv, sgu_g.reshape(1, WIDTH), w_s, jnp.transpose(b_s), yb, wa, wb, gates, gates)


def _out_proj_kernel(a_ref, w_ref, r_ref, g_ref, x_ref, h_ref):
    x = r_ref[...] + _dot(a_ref[...], w_ref[...])
    x_ref[...] = x
    h_ref[...] = _rms_rows(x, g_ref[...]).astype(h_ref.dtype)


def _out_proj(a, w, r, g, tm):
    t, k = a.shape
    d = w.shape[1]
    return pl.pallas_call(
        _out_proj_kernel,
        grid=(t // tm,),
        in_specs=[pl.BlockSpec((tm, k), lambda i: (i, 0)),
                  pl.BlockSpec((k, d), lambda i: (0, 0)),
                  pl.BlockSpec((tm, d), lambda i: (i, 0)),
                  pl.BlockSpec((1, d), lambda i: (0, 0))],
        out_specs=[pl.BlockSpec((tm, d), lambda i: (i, 0)),
                   pl.BlockSpec((tm, d), lambda i: (i, 0))],
        out_shape=[jax.ShapeDtypeStruct((t, d), F32), jax.ShapeDtypeStruct((t, d), BF16)],
        compiler_params=_params(("parallel",)),
        name="out_proj",
    )(a, w, r, g.reshape(1, d))


def _ffn_in_kernel(h_ref, wg_ref, wu_ref, o_ref, wgb_ref, wub_ref):
    _cast_weight_once(wg_ref, wgb_ref, 1)
    _cast_weight_once(wu_ref, wub_ref, 1)
    h = h_ref[...]
    gate = _dot(h, wgb_ref[...])
    up = _dot(h, wub_ref[...])
    o_ref[...] = (gate * _sigmoid(gate) * up).astype(o_ref.dtype)


def _ffn_in(h, w, hidden, tm, tn):
    t, d = h.shape
    tn = _tile(hidden, tn)
    nb = hidden // tn
    return pl.pallas_call(
        _ffn_in_kernel,
        grid=(nb, t // tm),
        in_specs=[pl.BlockSpec((tm, d), lambda j, i: (i, 0)),
                  pl.BlockSpec((d, tn), lambda j, i: (0, j)),
                  pl.BlockSpec((d, tn), lambda j, i: (0, j + nb))],
        out_specs=pl.BlockSpec((tm, tn), lambda j, i: (i, j)),
        out_shape=jax.ShapeDtypeStruct((t, hidden), BF16),
        scratch_shapes=[pltpu.VMEM((d, tn), BF16), pltpu.VMEM((d, tn), BF16)],
        compiler_params=_params(("parallel", "arbitrary")),
        name="swiglu_in",
    )(h, w, w)


def _matmul_residual_kernel(a_ref, w_ref, r_ref, o_ref):
    o_ref[...] = r_ref[...] + _dot(a_ref[...], w_ref[...])


def _matmul_residual(a, w, r, tm, tn, name):
    t, k = a.shape
    d = w.shape[1]
    tn = _tile(d, tn)
    return pl.pallas_call(
        _matmul_residual_kernel,
        grid=(t // tm, d // tn),
        in_specs=[pl.BlockSpec((tm, k), lambda i, j: (i, 0)),
                  pl.BlockSpec((k, tn), lambda i, j: (0, j)),
                  pl.BlockSpec((tm, tn), lambda i, j: (i, j))],
        out_specs=pl.BlockSpec((tm, tn), lambda i, j: (i, j)),
        out_shape=jax.ShapeDtypeStruct((t, d), F32),
        compiler_params=_params(("parallel", "parallel")),
        name=name,
    )(a, w, r)


def _ple_kernel(x_ref, g_ref, wg_ref, p_ref, wp_ref, o_ref):
    x = x_ref[...]
    h = _rms_rows(x, g_ref[...]).astype(BF16)
    gate = jax.nn.sigmoid(_dot(h, wg_ref[...]))
    ple = _dot(p_ref[...].astype(BF16), wp_ref[...])
    o_ref[...] = x + gate * ple


def _ple(x, g, wg, p, wp, tm):
    t, d = x.shape
    pd = p.shape[1]
    return pl.pallas_call(
        _ple_kernel,
        grid=(t // tm,),
        in_specs=[pl.BlockSpec((tm, d), lambda i: (i, 0)),
                  pl.BlockSpec((1, d), lambda i: (0, 0)),
                  pl.BlockSpec((d, d), lambda i: (0, 0)),
                  pl.BlockSpec((tm, pd), lambda i: (i, 0)),
                  pl.BlockSpec((pd, d), lambda i: (0, 0))],
        out_specs=pl.BlockSpec((tm, d), lambda i: (i, 0)),
        out_shape=jax.ShapeDtypeStruct((t, d), F32),
        compiler_params=_params(("parallel",)),
        name="gated_ple",
    )(x, g.reshape(1, d), wg, p, wp)


def _layer(x, p, attn_norm_g, w_in, sgu_norm_g, w_s, b_s, q_norm_g, k_norm_g,
           w_up_a, w_up_b, w_o, ffn_norm_g, w_ffn_in, w_ffn_out,
           ple_norm_g, w_ple_gate, w_ple, batch, seq):
    t, d = x.shape
    hidden = w_ffn_out.shape[0]
    tm = _tile(t, 1024)
    tm_full = _tile(t, 512)
    tm_big = _tile(t, 2048)
    blk = _tile(seq, 256)
    nsub = 8 if seq % (8 * blk) == 0 else 1

    c_qk = 2 * WIDTH
    c_v = c_qk + 2 * WIDTH
    c_gate = c_v + WIDTH

    uv, h = _proj_sgu(x, attn_norm_g, w_in, c_qk, tm_full)
    scale = GROUP ** -0.5
    qk_gain = jnp.concatenate([jnp.tile(q_norm_g * scale, N_GROUPS),
                               jnp.tile(k_norm_g, N_GROUPS)]).reshape(1, 2 * WIDTH)
    qk = _proj(h, w_in, c_qk, 2 * WIDTH, "headnorm", qk_gain, tm, 2048, "proj_qk")
    v_att = _proj(h, w_in, c_v, WIDTH, "none", None, tm_big, 1024, "proj_v")
    gates = _proj(h, w_in, c_gate, 2 * d, "sigmoid", None, tm, 1024, "proj_gates")

    y_b = _attention(qk, v_att, batch, seq, blk, nsub)

    merged = _merge(uv, sgu_norm_g.reshape(-1), w_s, b_s, y_b,
                    w_up_a.astype(BF16), w_up_b.astype(BF16), gates, tm_full)
    x, h = _out_proj(merged, w_o.astype(BF16), x, ffn_norm_g, tm_full)

    act = _ffn_in(h, w_ffn_in, hidden, tm, 512)
    x = _matmul_residual(act, w_ffn_out.astype(BF16), x, tm, 512, "swiglu_out")

    return _ple(x, ple_norm_g, w_ple_gate.astype(BF16), p, w_ple.astype(BF16), tm_full)


def kernel(x, p, attn_norm_g, w_in, sgu_norm_g, w_s, b_s, q_norm_g, k_norm_g, w_up_a, w_up_b, w_o,
           ffn_norm_g, w_ffn_in, w_ffn_out, ple_norm_g, w_ple_gate, w_ple):
    batch, seq, d = x.shape
    xf = x.reshape(batch * seq, d)
    for i in range(p.shape[0]):
        xf = _layer(xf, p[i].reshape(batch * seq, -1), attn_norm_g[i], w_in[i], sgu_norm_g[i], w_s[i],
                    b_s[i], q_norm_g[i], k_norm_g[i], w_up_a[i], w_up_b[i], w_o[i], ffn_norm_g[i],
                    w_ffn_in[i], w_ffn_out[i], ple_norm_g[i], w_ple_gate[i], w_ple[i], batch, seq)
    return xf.reshape(batch, seq, d)
```

```python
import functools

import jax
import jax.numpy as jnp
from jax import lax
from jax.experimental import pallas as pl
from jax.experimental.pallas import tpu as pltpu

F32 = jnp.float32
BF16 = jnp.bfloat16

EPS = 1e-6
GROUP = 128
N_GROUPS = 8
WIDTH = GROUP * N_GROUPS

V7X_VMEM_BYTES = 64 * 1024 * 1024
VMEM_LIMIT = V7X_VMEM_BYTES - 8 * 1024 * 1024

EXP_UNDERFLOW = -104.0


def _tile(n, want):
    if n <= want:
        return n
    t = (want // 128) * 128
    while t >= 128:
        if n % t == 0:
            return t
        t -= 128
    return n


def _params(sem):
    return pltpu.CompilerParams(dimension_semantics=sem, vmem_limit_bytes=VMEM_LIMIT)


def _dot(a, b):
    return jnp.dot(a, b, preferred_element_type=F32)


def _sigmoid(x):
    return 0.5 * jnp.tanh(0.5 * x) + 0.5


def _rms_rows(x, g):
    y = x * lax.rsqrt(jnp.mean(x * x, axis=-1, keepdims=True) + EPS)
    return y * g


def _cast_weight_once(w_ref, wb_ref, axis):
    @pl.when(pl.program_id(axis) == 0)
    def _():
        wb_ref[...] = w_ref[...].astype(wb_ref.dtype)


def _proj_sgu_kernel(x_ref, g_ref, w_ref, o_ref, h_ref, wb_ref):
    _cast_weight_once(w_ref, wb_ref, 0)
    h = _rms_rows(x_ref[...], g_ref[...]).astype(h_ref.dtype)
    h_ref[...] = h
    o_ref[...] = jax.nn.gelu(_dot(h, wb_ref[...])).astype(o_ref.dtype)


def _proj_sgu(x, g, w, ncols, tm):
    t, d = x.shape
    return pl.pallas_call(
        _proj_sgu_kernel,
        grid=(t // tm,),
        in_specs=[pl.BlockSpec((tm, d), lambda i: (i, 0)),
                  pl.BlockSpec((1, d), lambda i: (0, 0)),
                  pl.BlockSpec((d, ncols), lambda i: (0, 0))],
        out_specs=[pl.BlockSpec((tm, ncols), lambda i: (i, 0)),
                   pl.BlockSpec((tm, d), lambda i: (i, 0))],
        out_shape=[jax.ShapeDtypeStruct((t, ncols), BF16), jax.ShapeDtypeStruct((t, d), BF16)],
        scratch_shapes=[pltpu.VMEM((d, ncols), BF16)],
        compiler_params=_params(("arbitrary",)),
        name="proj_sgu",
    )(x, g.reshape(1, d), w)


def _proj_kernel(h_ref, w_ref, gain_ref, o_ref, wb_ref, *, kind):
    _cast_weight_once(w_ref, wb_ref, 1)
    acc = _dot(h_ref[...], wb_ref[...])
    if kind == "sigmoid":
        o_ref[...] = _sigmoid(acc).astype(o_ref.dtype)
    elif kind == "none":
        o_ref[...] = acc.astype(o_ref.dtype)
    elif kind == "headnorm":
        gain = gain_ref[...]
        for c in range(acc.shape[1] // GROUP):
            sl = slice(c * GROUP, (c + 1) * GROUP)
            o_ref[:, sl] = _rms_rows(acc[:, sl], gain[:, sl]).astype(o_ref.dtype)
    else:
        raise ValueError(kind)


def _proj(h, w, col0, ncols, kind, gain, tm, tn, name):
    t, d = h.shape
    tn = _tile(ncols, tn)
    assert col0 % tn == 0
    off = col0 // tn
    if gain is None:
        gain = jnp.ones((1, ncols), F32)
    return pl.pallas_call(
        functools.partial(_proj_kernel, kind=kind),
        grid=(ncols // tn, t // tm),
        in_specs=[pl.BlockSpec((tm, d), lambda j, i: (i, 0)),
                  pl.BlockSpec((d, tn), lambda j, i: (0, j + off)),
                  pl.BlockSpec((1, tn), lambda j, i: (0, j))],
        out_specs=pl.BlockSpec((tm, tn), lambda j, i: (i, j)),
        out_shape=jax.ShapeDtypeStruct((t, ncols), BF16),
        scratch_shapes=[pltpu.VMEM((d, tn), BF16)],
        compiler_params=_params(("parallel", "arbitrary")),
        name=name,
    )(h, w, gain)


def _attn_kernel(q_ref, k_ref, v_ref, o_ref, *, blk, nsub):
    i0 = pl.program_id(2) * nsub
    row = lax.broadcasted_iota(jnp.int32, (blk, blk), 0)
    col = lax.broadcasted_iota(jnp.int32, (blk, blk), 1)
    strict = col < row
    later = strict.astype(BF16)

    def tile(q, j, carry, diagonal):
        start = pl.multiple_of(j * blk, blk)
        k = k_ref[pl.ds(start, blk), :]
        v = v_ref[pl.ds(start, blk), :]
        z = lax.dot_general(q, k, (((1,), (1,)), ((), ())), preferred_element_type=F32)
        log_beta = jnp.minimum(z, 0.0) - jnp.log(1.0 + jnp.exp(-jnp.abs(z)))
        log_1m = log_beta - z
        if diagonal:
            log_1m = jnp.where(strict, log_1m, 0.0)
        tail = _dot(log_1m.astype(BF16), later) + carry
        w = jnp.exp(log_beta + tail)
        if diagonal:
            w = jnp.where(strict, w, 0.0)
        return _dot(w.astype(BF16), v), jnp.sum(log_1m, axis=1, keepdims=True)

    def step(n, carries, accs, diagonal):
        new_c, new_a = [], []
        for r in range(nsub):
            q = q_ref[r * blk:(r + 1) * blk, :]
            j = i0 + r - n
            if diagonal:
                out, rowsum = tile(q, j, carries[r], True)
            else:
                valid = j >= 0
                out, rowsum = tile(q, jnp.maximum(j, 0),
                                   carries[r] + jnp.where(valid, 0.0, -1e30), False)
                rowsum = jnp.where(valid, rowsum, 0.0)
            new_a.append(accs[r] + out)
            new_c.append(carries[r] + rowsum)
        return tuple(new_c), tuple(new_a)

    carries = tuple(jnp.zeros((blk, 1), F32) for _ in range(nsub))
    accs = tuple(jnp.zeros((blk, GROUP), F32) for _ in range(nsub))
    carries, accs = step(0, carries, accs, True)
    carries, accs = step(1, carries, accs, False)

    def cond(state):
        n, carries, _ = state
        worst = jnp.full((blk, 1), -1e30, F32)
        for r in range(nsub):
            worst = jnp.maximum(worst, jnp.where(i0 + r - n >= 0, carries[r], -1e30))
        return jnp.max(worst) > EXP_UNDERFLOW

    def body(state):
        n, carries, accs = state
        carries, accs = step(n, carries, accs, False)
        return n + 1, carries, accs

    _, _, accs = lax.while_loop(cond, body, (jnp.int32(2), carries, accs))
    for r in range(nsub):
        o_ref[r * blk:(r + 1) * blk, :] = accs[r].astype(o_ref.dtype)


def _attention(qk, v, batch, seq, blk, nsub):
    t = qk.shape[0]
    rows = blk * nsub
    nq = seq // rows
    return pl.pallas_call(
        functools.partial(_attn_kernel, blk=blk, nsub=nsub),
        grid=(batch, N_GROUPS, nq),
        in_specs=[pl.BlockSpec((rows, GROUP), lambda b, h, i: (b * nq + i, h)),
                  pl.BlockSpec((seq, GROUP), lambda b, h, i: (b, N_GROUPS + h)),
                  pl.BlockSpec((seq, GROUP), lambda b, h, i: (b, h))],
        out_specs=pl.BlockSpec((rows, GROUP), lambda b, h, i: (b * nq + i, h)),
        out_shape=jax.ShapeDtypeStruct((t, WIDTH), BF16),
        compiler_params=_params(("parallel", "parallel", "arbitrary")),
        name="stick_breaking_attention",
    )(qk, qk, v)


def _sgu_tile(u_ref, v_ref, g_ref, ws_ref, b_ref, y_ref, tm):
    nchunk = tm // GROUP
    row = lax.broadcasted_iota(jnp.int32, (GROUP, GROUP), 0)
    col = lax.broadcasted_iota(jnp.int32, (GROUP, GROUP), 1)
    causal = row >= col
    for g in range(N_GROUPS):
        sl = slice(g * GROUP, (g + 1) * GROUP)
        vn = _rms_rows(v_ref[:, sl].astype(F32), g_ref[:, sl]).astype(BF16)
        ws = jnp.where(causal, ws_ref[g], 0.0).astype(BF16)
        vcat = jnp.concatenate([vn[c * GROUP:(c + 1) * GROUP, :] for c in range(nchunk)], axis=1)
        mixed = _dot(ws, vcat) + b_ref[:, g:g + 1]
        for c in range(nchunk):
            rows = slice(c * GROUP, (c + 1) * GROUP)
            u = u_ref[rows, sl].astype(F32)
            y_ref[rows, sl] = (u * mixed[:, c * GROUP:(c + 1) * GROUP]).astype(y_ref.dtype)


def _merge_kernel(u_ref, v_ref, sg_ref, ws_ref, b_ref, yb_ref, wa_ref, wb_ref, ga_ref, gb_ref, o_ref,
                  ya_ref, *, tm):
    _sgu_tile(u_ref, v_ref, sg_ref, ws_ref, b_ref, ya_ref, tm)
    b = _dot(yb_ref[...], wb_ref[...])
    a = _dot(ya_ref[...], wa_ref[...])
    o_ref[...] = (ga_ref[...].astype(F32) * a + gb_ref[...].astype(F32) * b).astype(o_ref.dtype)


def _merge(uv, sgu_g, w_s, b_s, yb, wa, wb, gates, tm):
    t = yb.shape[0]
    d = wa.shape[1]
    return pl.pallas_call(
        functools.partial(_merge_kernel, tm=tm),
        grid=(t // tm,),
        in_specs=[pl.BlockSpec((tm, WIDTH), lambda i: (i, 0)),
                  pl.BlockSpec((tm, WIDTH), lambda i: (i, 1)),
                  pl.BlockSpec((1, WIDTH), lambda i: (0, 0)),
                  pl.BlockSpec((N_GROUPS, GROUP, GROUP), lambda i: (0, 0, 0)),
                  pl.BlockSpec((GROUP, N_GROUPS), lambda i: (0, 0)),
                  pl.BlockSpec((tm, WIDTH), lambda i: (i, 0)),
                  pl.BlockSpec((WIDTH, d), lambda i: (0, 0)),
                  pl.BlockSpec((WIDTH, d), lambda i: (0, 0)),
                  pl.BlockSpec((tm, d), lambda i: (i, 0)),
                  pl.BlockSpec((tm, d), lambda i: (i, 1))],
        out_specs=pl.BlockSpec((tm, d), lambda i: (i, 0)),
        out_shape=jax.ShapeDtypeStruct((t, d), BF16),
        scratch_shapes=[pltpu.VMEM((tm, WIDTH), BF16)],
        compiler_params=_params(("parallel",)),
        name="sgu_gated_merge",
    )(uv, uv, sgu_g.reshape(1, WIDTH), w_s, jnp.transpose(b_s), yb, wa, wb, gates, gates)


def _out_proj_kernel(a_ref, w_ref, r_ref, g_ref, x_ref, h_ref):
    x = r_ref[...] + _dot(a_ref[...], w_ref[...])
    x_ref[...] = x
    h_ref[...] = _rms_rows(x, g_ref[...]).astype(h_ref.dtype)


def _out_proj(a, w, r, g, tm):
    t, k = a.shape
    d = w.shape[1]
    return pl.pallas_call(
        _out_proj_kernel,
        grid=(t // tm,),
        in_specs=[pl.BlockSpec((tm, k), lambda i: (i, 0)),
                  pl.BlockSpec((k, d), lambda i: (0, 0)),
                  pl.BlockSpec((tm, d), lambda i: (i, 0)),
                  pl.BlockSpec((1, d), lambda i: (0, 0))],
        out_specs=[pl.BlockSpec((tm, d), lambda i: (i, 0)),
                   pl.BlockSpec((tm, d), lambda i: (i, 0))],
        out_shape=[jax.ShapeDtypeStruct((t, d), F32), jax.ShapeDtypeStruct((t, d), BF16)],
        compiler_params=_params(("parallel",)),
        name="out_proj",
    )(a, w, r, g.reshape(1, d))


def _ffn_in_kernel(h_ref, wg_ref, wu_ref, o_ref, wgb_ref, wub_ref):
    _cast_weight_once(wg_ref, wgb_ref, 1)
    _cast_weight_once(wu_ref, wub_ref, 1)
    h = h_ref[...]
    gate = _dot(h, wgb_ref[...])
    up = _dot(h, wub_ref[...])
    o_ref[...] = (gate * _sigmoid(gate) * up).astype(o_ref.dtype)


def _ffn_in(h, w, hidden, tm, tn):
    t, d = h.shape
    tn = _tile(hidden, tn)
    nb = hidden // tn
    return pl.pallas_call(
        _ffn_in_kernel,
        grid=(nb, t // tm),
        in_specs=[pl.BlockSpec((tm, d), lambda j, i: (i, 0)),
                  pl.BlockSpec((d, tn), lambda j, i: (0, j)),
                  pl.BlockSpec((d, tn), lambda j, i: (0, j + nb))],
        out_specs=pl.BlockSpec((tm, tn), lambda j, i: (i, j)),
        out_shape=jax.ShapeDtypeStruct((t, hidden), BF16),
        scratch_shapes=[pltpu.VMEM((d, tn), BF16), pltpu.VMEM((d, tn), BF16)],
        compiler_params=_params(("parallel", "arbitrary")),
        name="swiglu_in",
    )(h, w, w)


def _matmul_residual_kernel(a_ref, w_ref, r_ref, o_ref):
    o_ref[...] = r_ref[...] + _dot(a_ref[...], w_ref[...])


def _matmul_residual(a, w, r, tm, tn, name):
    t, k = a.shape
    d = w.shape[1]
    tn = _tile(d, tn)
    return pl.pallas_call(
        _matmul_residual_kernel,
        grid=(t // tm, d // tn),
        in_specs=[pl.BlockSpec((tm, k), lambda i, j: (i, 0)),
                  pl.BlockSpec((k, tn), lambda i, j: (0, j)),
                  pl.BlockSpec((tm, tn), lambda i, j: (i, j))],
        out_specs=pl.BlockSpec((tm, tn), lambda i, j: (i, j)),
        out_shape=jax.ShapeDtypeStruct((t, d), F32),
        compiler_params=_params(("parallel", "parallel")),
        name=name,
    )(a, w, r)


def _ple_kernel(x_ref, g_ref, wg_ref, p_ref, wp_ref, o_ref):
    x = x_ref[...]
    h = _rms_rows(x, g_ref[...]).astype(BF16)
    gate = jax.nn.sigmoid(_dot(h, wg_ref[...]))
    ple = _dot(p_ref[...].astype(BF16), wp_ref[...])
    o_ref[...] = x + gate * ple


def _ple(x, g, wg, p, wp, tm):
    t, d = x.shape
    pd = p.shape[1]
    return pl.pallas_call(
        _ple_kernel,
        grid=(t // tm,),
        in_specs=[pl.BlockSpec((tm, d), lambda i: (i, 0)),
                  pl.BlockSpec((1, d), lambda i: (0, 0)),
                  pl.BlockSpec((d, d), lambda i: (0, 0)),
                  pl.BlockSpec((tm, pd), lambda i: (i, 0)),
                  pl.BlockSpec((pd, d), lambda i: (0, 0))],
        out_specs=pl.BlockSpec((tm, d), lambda i: (i, 0)),
        out_shape=jax.ShapeDtypeStruct((t, d), F32),
        compiler_params=_params(("parallel",)),
        name="gated_ple",
    )(x, g.reshape(1, d), wg, p, wp)


def _layer(x, p, attn_norm_g, w_in, sgu_norm_g, w_s, b_s, q_norm_g, k_norm_g,
           w_up_a, w_up_b, w_o, ffn_norm_g, w_ffn_in, w_ffn_out,
           ple_norm_g, w_ple_gate, w_ple, batch, seq):
    t, d = x.shape
    hidden = w_ffn_out.shape[0]
    tm = _tile(t, 1024)
    tm_full = _tile(t, 512)
    tm_big = _tile(t, 2048)
    blk = _tile(seq, 256)
    nsub = next(n for n in (16, 8, 4, 2, 1) if seq % (n * blk) == 0)

    c_qk = 2 * WIDTH
    c_v = c_qk + 2 * WIDTH
    c_gate = c_v + WIDTH

    uv, h = _proj_sgu(x, attn_norm_g, w_in, c_qk, tm_full)
    scale = GROUP ** -0.5
    qk_gain = jnp.concatenate([jnp.tile(q_norm_g * scale, N_GROUPS),
                               jnp.tile(k_norm_g, N_GROUPS)]).reshape(1, 2 * WIDTH)
    qk = _proj(h, w_in, c_qk, 2 * WIDTH, "headnorm", qk_gain, tm, 2048, "proj_qk")
    v_att = _proj(h, w_in, c_v, WIDTH, "none", None, tm_big, 1024, "proj_v")
    gates = _proj(h, w_in, c_gate, 2 * d, "sigmoid", None, tm, 1024, "proj_gates")

    y_b = _attention(qk, v_att, batch, seq, blk, nsub)

    merged = _merge(uv, sgu_norm_g.reshape(-1), w_s, b_s, y_b,
                    w_up_a.astype(BF16), w_up_b.astype(BF16), gates, tm_full)
    x, h = _out_proj(merged, w_o.astype(BF16), x, ffn_norm_g, tm_full)

    act = _ffn_in(h, w_ffn_in, hidden, tm, 512)
    x = _matmul_residual(act, w_ffn_out.astype(BF16), x, tm, 512, "swiglu_out")

    return _ple(x, ple_norm_g, w_ple_gate.astype(BF16), p, w_ple.astype(BF16), tm_full)


def kernel(x, p, attn_norm_g, w_in, sgu_norm_g, w_s, b_s, q_norm_g, k_norm_g, w_up_a, w_up_b, w_o,
           ffn_norm_g, w_ffn_in, w_ffn_out, ple_norm_g, w_ple_gate, w_ple):
    batch, seq, d = x.shape
    xf = x.reshape(batch * seq, d)
    for i in range(p.shape[0]):
        xf = _layer(xf, p[i].reshape(batch * seq, -1), attn_norm_g[i], w_in[i], sgu_norm_g[i], w_s[i],
                    b_s[i], q_norm_g[i], k_norm_g[i], w_up_a[i], w_up_b[i], w_o[i], ffn_norm_g[i],
                    w_ffn_in[i], w_ffn_out[i], ple_norm_g[i], w_ple_gate[i], w_ple[i], batch, seq)
    return xf.reshape(batch, seq, d)
```

```python
import functools

import jax
import jax.numpy as jnp
from jax import lax
from jax.experimental import pallas as pl
from jax.experimental.pallas import tpu as pltpu

F32 = jnp.float32
BF16 = jnp.bfloat16

EPS = 1e-6
GROUP = 128
N_GROUPS = 8
WIDTH = GROUP * N_GROUPS

V7X_VMEM_BYTES = 64 * 1024 * 1024
VMEM_LIMIT = V7X_VMEM_BYTES - 8 * 1024 * 1024

EXP_UNDERFLOW = -104.0


def _tile(n, want):
    if n <= want:
        return n
    t = (want // 128) * 128
    while t >= 128:
        if n % t == 0:
            return t
        t -= 128
    return n


def _params(sem):
    return pltpu.CompilerParams(dimension_semantics=sem, vmem_limit_bytes=VMEM_LIMIT)


def _dot(a, b):
    return jnp.dot(a, b, preferred_element_type=F32)


def _sigmoid(x):
    return 0.5 * jnp.tanh(0.5 * x) + 0.5


def _rms_rows(x, g):
    y = x * lax.rsqrt(jnp.mean(x * x, axis=-1, keepdims=True) + EPS)
    return y * g


def _cast_weight_once(w_ref, wb_ref, axis):
    @pl.when(pl.program_id(axis) == 0)
    def _():
        wb_ref[...] = w_ref[...].astype(wb_ref.dtype)


def _proj_sgu_kernel(x_ref, g_ref, w_ref, o_ref, h_ref, wb_ref):
    _cast_weight_once(w_ref, wb_ref, 0)
    h = _rms_rows(x_ref[...], g_ref[...]).astype(h_ref.dtype)
    h_ref[...] = h
    o_ref[...] = jax.nn.gelu(_dot(h, wb_ref[...])).astype(o_ref.dtype)


def _proj_sgu(x, g, w, ncols, tm):
    t, d = x.shape
    return pl.pallas_call(
        _proj_sgu_kernel,
        grid=(t // tm,),
        in_specs=[pl.BlockSpec((tm, d), lambda i: (i, 0)),
                  pl.BlockSpec((1, d), lambda i: (0, 0)),
                  pl.BlockSpec((d, ncols), lambda i: (0, 0))],
        out_specs=[pl.BlockSpec((tm, ncols), lambda i: (i, 0)),
                   pl.BlockSpec((tm, d), lambda i: (i, 0))],
        out_shape=[jax.ShapeDtypeStruct((t, ncols), BF16), jax.ShapeDtypeStruct((t, d), BF16)],
        scratch_shapes=[pltpu.VMEM((d, ncols), BF16)],
        compiler_params=_params(("arbitrary",)),
        name="proj_sgu",
    )(x, g.reshape(1, d), w)


def _proj_kernel(h_ref, w_ref, gain_ref, o_ref, wb_ref, *, kind):
    _cast_weight_once(w_ref, wb_ref, 1)
    acc = _dot(h_ref[...], wb_ref[...])
    if kind == "sigmoid":
        o_ref[...] = _sigmoid(acc).astype(o_ref.dtype)
    elif kind == "none":
        o_ref[...] = acc.astype(o_ref.dtype)
    elif kind == "headnorm":
        gain = gain_ref[...]
        for c in range(acc.shape[1] // GROUP):
            sl = slice(c * GROUP, (c + 1) * GROUP)
            o_ref[:, sl] = _rms_rows(acc[:, sl], gain[:, sl]).astype(o_ref.dtype)
    else:
        raise ValueError(kind)


def _proj(h, w, col0, ncols, kind, gain, tm, tn, name):
    t, d = h.shape
    tn = _tile(ncols, tn)
    assert col0 % tn == 0
    off = col0 // tn
    if gain is None:
        gain = jnp.ones((1, ncols), F32)
    return pl.pallas_call(
        functools.partial(_proj_kernel, kind=kind),
        grid=(ncols // tn, t // tm),
        in_specs=[pl.BlockSpec((tm, d), lambda j, i: (i, 0)),
                  pl.BlockSpec((d, tn), lambda j, i: (0, j + off)),
                  pl.BlockSpec((1, tn), lambda j, i: (0, j))],
        out_specs=pl.BlockSpec((tm, tn), lambda j, i: (i, j)),
        out_shape=jax.ShapeDtypeStruct((t, ncols), BF16),
        scratch_shapes=[pltpu.VMEM((d, tn), BF16)],
        compiler_params=_params(("parallel", "arbitrary")),
        name=name,
    )(h, w, gain)


def _attn_kernel(q_ref, k_ref, v_ref, o_ref, *, blk, nsub):
    i0 = pl.program_id(2) * nsub
    row = lax.broadcasted_iota(jnp.int32, (blk, blk), 0)
    col = lax.broadcasted_iota(jnp.int32, (blk, blk), 1)
    strict = col < row
    later = strict.astype(BF16)
    row2 = lax.broadcasted_iota(jnp.int32, (2 * blk, blk), 0)
    col2 = lax.broadcasted_iota(jnp.int32, (2 * blk, blk), 1)
    strict2 = jnp.logical_or(col2 < row2, row2 >= blk)

    def scores(q, j):
        start = pl.multiple_of(j * blk, blk)
        k = k_ref[pl.ds(start, blk), :]
        z = lax.dot_general(q, k, (((1,), (1,)), ((), ())), preferred_element_type=F32)
        log_beta = jnp.minimum(z, 0.0) - jnp.log(1.0 + jnp.exp(-jnp.abs(z)))
        return log_beta, log_beta - z

    def weights(log_beta, log_1m, carry, j):
        start = pl.multiple_of(j * blk, blk)
        v = v_ref[pl.ds(start, blk), :]
        tail = _dot(log_1m.astype(BF16), later) + carry
        return jnp.exp(log_beta + tail), v

    def tile(q, j, carry, mask):
        log_beta, log_1m = scores(q, j)
        if mask is not None:
            log_1m = jnp.where(mask, log_1m, 0.0)
        w, v = weights(log_beta, log_1m, carry, j)
        if mask is not None:
            w = jnp.where(mask, w, 0.0)
        return _dot(w.astype(BF16), v), jnp.sum(log_1m, axis=1, keepdims=True)

    zero_c = jnp.zeros((blk, 1), F32)

    lb, l1 = [], []
    for r in range(nsub - 1):
        log_beta, log_1m = scores(q_ref[r * blk:(r + 2) * blk, :], i0 + r)
        lb.append(log_beta)
        l1.append(jnp.where(strict2, log_1m, 0.0))
    sums = [jnp.sum(x, axis=1, keepdims=True) for x in l1]
    last = nsub - 1
    out_last, sum_last = tile(q_ref[last * blk:(last + 1) * blk, :], i0 + last, zero_c, strict)
    diag_sum = [s[:blk] for s in sums] + [sum_last]
    prev_valid = i0 >= 1
    out_first, sum_first = tile(q_ref[0:blk, :], jnp.maximum(i0 - 1, 0),
                                diag_sum[0] + jnp.where(prev_valid, 0.0, -1e30), None)
    sum_first = jnp.where(prev_valid, sum_first, 0.0)
    outs = []
    for r in range(nsub - 1):
        carry2 = jnp.concatenate([zero_c, diag_sum[r + 1]], axis=0)
        w, v = weights(lb[r], l1[r], carry2, i0 + r)
        w = jnp.where(strict2, w, 0.0)
        outs.append(_dot(w.astype(BF16), v))
    accs, carries = [], []
    for r in range(nsub):
        diag_out = outs[r][:blk] if r < nsub - 1 else out_last
        prev_out = outs[r - 1][blk:] if r >= 1 else out_first
        prev_sum = sums[r - 1][blk:] if r >= 1 else sum_first
        accs.append(diag_out + prev_out)
        carries.append(diag_sum[r] + prev_sum)

    def step(n, carries, accs):
        new_c, new_a = [], []
        for r in range(nsub):
            q = q_ref[r * blk:(r + 1) * blk, :]
            j = i0 + r - n
            valid = j >= 0
            out, rowsum = tile(q, jnp.maximum(j, 0),
                               carries[r] + jnp.where(valid, 0.0, -1e30), None)
            new_a.append(accs[r] + out)
            new_c.append(carries[r] + jnp.where(valid, rowsum, 0.0))
        return tuple(new_c), tuple(new_a)

    def cond(state):
        n, carries, _ = state
        worst = jnp.full((blk, 1), -1e30, F32)
        for r in range(nsub):
            worst = jnp.maximum(worst, jnp.where(i0 + r - n >= 0, carries[r], -1e30))
        return jnp.max(worst) > EXP_UNDERFLOW

    def body(state):
        n, carries, accs = state
        carries, accs = step(n, carries, accs)
        return n + 1, carries, accs

    _, _, accs = lax.while_loop(cond, body, (jnp.int32(2), tuple(carries), tuple(accs)))
    for r in range(nsub):
        o_ref[r * blk:(r + 1) * blk, :] = accs[r].astype(o_ref.dtype)


def _attention(qk, v, batch, seq, blk, nsub):
    t = qk.shape[0]
    rows = blk * nsub
    nq = seq // rows
    return pl.pallas_call(
        functools.partial(_attn_kernel, blk=blk, nsub=nsub),
        grid=(batch, N_GROUPS, nq),
        in_specs=[pl.BlockSpec((rows, GROUP), lambda b, h, i: (b * nq + i, h)),
                  pl.BlockSpec((seq, GROUP), lambda b, h, i: (b, N_GROUPS + h)),
                  pl.BlockSpec((seq, GROUP), lambda b, h, i: (b, h))],
        out_specs=pl.BlockSpec((rows, GROUP), lambda b, h, i: (b * nq + i, h)),
        out_shape=jax.ShapeDtypeStruct((t, WIDTH), BF16),
        compiler_params=_params(("parallel", "parallel", "arbitrary")),
        name="stick_breaking_attention",
    )(qk, qk, v)


def _sgu_tile(u_ref, v_ref, g_ref, ws_ref, b_ref, y_ref, tm):
    nchunk = tm // GROUP
    row = lax.broadcasted_iota(jnp.int32, (GROUP, GROUP), 0)
    col = lax.broadcasted_iota(jnp.int32, (GROUP, GROUP), 1)
    causal = row >= col
    for g in range(N_GROUPS):
        sl = slice(g * GROUP, (g + 1) * GROUP)
        vn = _rms_rows(v_ref[:, sl].astype(F32), g_ref[:, sl]).astype(BF16)
        ws = jnp.where(causal, ws_ref[g], 0.0).astype(BF16)
        vcat = jnp.concatenate([vn[c * GROUP:(c + 1) * GROUP, :] for c in range(nchunk)], axis=1)
        mixed = _dot(ws, vcat) + b_ref[:, g:g + 1]
        for c in range(nchunk):
            rows = slice(c * GROUP, (c + 1) * GROUP)
            u = u_ref[rows, sl].astype(F32)
            y_ref[rows, sl] = (u * mixed[:, c * GROUP:(c + 1) * GROUP]).astype(y_ref.dtype)


def _merge_kernel(u_ref, v_ref, sg_ref, ws_ref, b_ref, yb_ref, wa_ref, wb_ref, ga_ref, gb_ref, o_ref,
                  ya_ref, *, tm):
    _sgu_tile(u_ref, v_ref, sg_ref, ws_ref, b_ref, ya_ref, tm)
    b = _dot(yb_ref[...], wb_ref[...])
    a = _dot(ya_ref[...], wa_ref[...])
    o_ref[...] = (ga_ref[...].astype(F32) * a + gb_ref[...].astype(F32) * b).astype(o_ref.dtype)


def _merge(uv, sgu_g, w_s, b_s, yb, wa, wb, gates, tm):
    t = yb.shape[0]
    d = wa.shape[1]
    return pl.pallas_call(
        functools.partial(_merge_kernel, tm=tm),
        grid=(t // tm,),
        in_specs=[pl.BlockSpec((tm, WIDTH), lambda i: (i, 0)),
                  pl.BlockSpec((tm, WIDTH), lambda i: (i, 1)),
                  pl.BlockSpec((1, WIDTH), lambda i: (0, 0)),
                  pl.BlockSpec((N_GROUPS, GROUP, GROUP), lambda i: (0, 0, 0)),
                  pl.BlockSpec((GROUP, N_GROUPS), lambda i: (0, 0)),
                  pl.BlockSpec((tm, WIDTH), lambda i: (i, 0)),
                  pl.BlockSpec((WIDTH, d), lambda i: (0, 0)),
                  pl.BlockSpec((WIDTH, d), lambda i: (0, 0)),
                  pl.BlockSpec((tm, d), lambda i: (i, 0)),
                  pl.BlockSpec((tm, d), lambda i: (i, 1))],
        out_specs=pl.BlockSpec((tm, d), lambda i: (i, 0)),
        out_shape=jax.ShapeDtypeStruct((t, d), BF16),
        scratch_shapes=[pltpu.VMEM((tm, WIDTH), BF16)],
        compiler_params=_params(("parallel",)),
        name="sgu_gated_merge",
    )(uv, uv, sgu_g.reshape(1, WIDTH), w_s, jnp.transpose(b_s), yb, wa, wb, gates, gates)


def _out_proj_kernel(a_ref, w_ref, r_ref, g_ref, x_ref, h_ref):
    x = r_ref[...] + _dot(a_ref[...], w_ref[...])
    x_ref[...] = x
    h_ref[...] = _rms_rows(x, g_ref[...]).astype(h_ref.dtype)


def _out_proj(a, w, r, g, tm):
    t, k = a.shape
    d = w.shape[1]
    return pl.pallas_call(
        _out_proj_kernel,
        grid=(t // tm,),
        in_specs=[pl.BlockSpec((tm, k), lambda i: (i, 0)),
                  pl.BlockSpec((k, d), lambda i: (0, 0)),
                  pl.BlockSpec((tm, d), lambda i: (i, 0)),
                  pl.BlockSpec((1, d), lambda i: (0, 0))],
        out_specs=[pl.BlockSpec((tm, d), lambda i: (i, 0)),
                   pl.BlockSpec((tm, d), lambda i: (i, 0))],
        out_shape=[jax.ShapeDtypeStruct((t, d), F32), jax.ShapeDtypeStruct((t, d), BF16)],
        compiler_params=_params(("parallel",)),
        name="out_proj",
    )(a, w, r, g.reshape(1, d))


def _ffn_in_kernel(h_ref, wg_ref, wu_ref, o_ref, wgb_ref, wub_ref):
    _cast_weight_once(wg_ref, wgb_ref, 1)
    _cast_weight_once(wu_ref, wub_ref, 1)
    h = h_ref[...]
    gate = _dot(h, wgb_ref[...])
    up = _dot(h, wub_ref[...])
    o_ref[...] = (gate * _sigmoid(gate) * up).astype(o_ref.dtype)


def _ffn_in(h, w, hidden, tm, tn):
    t, d = h.shape
    tn = _tile(hidden, tn)
    nb = hidden // tn
    return pl.pallas_call(
        _ffn_in_kernel,
        grid=(nb, t // tm),
        in_specs=[pl.BlockSpec((tm, d), lambda j, i: (i, 0)),
                  pl.BlockSpec((d, tn), lambda j, i: (0, j)),
                  pl.BlockSpec((d, tn), lambda j, i: (0, j + nb))],
        out_specs=pl.BlockSpec((tm, tn), lambda j, i: (i, j)),
        out_shape=jax.ShapeDtypeStruct((t, hidden), BF16),
        scratch_shapes=[pltpu.VMEM((d, tn), BF16), pltpu.VMEM((d, tn), BF16)],
        compiler_params=_params(("parallel", "arbitrary")),
        name="swiglu_in",
    )(h, w, w)


def _matmul_residual_kernel(a_ref, w_ref, r_ref, o_ref):
    o_ref[...] = r_ref[...] + _dot(a_ref[...], w_ref[...])


def _matmul_residual(a, w, r, tm, tn, name):
    t, k = a.shape
    d = w.shape[1]
    tn = _tile(d, tn)
    return pl.pallas_call(
        _matmul_residual_kernel,
        grid=(t // tm, d // tn),
        in_specs=[pl.BlockSpec((tm, k), lambda i, j: (i, 0)),
                  pl.BlockSpec((k, tn), lambda i, j: (0, j)),
                  pl.BlockSpec((tm, tn), lambda i, j: (i, j))],
        out_specs=pl.BlockSpec((tm, tn), lambda i, j: (i, j)),
        out_shape=jax.ShapeDtypeStruct((t, d), F32),
        compiler_params=_params(("parallel", "parallel")),
        name=name,
    )(a, w, r)


def _ple_kernel(x_ref, g_ref, wg_ref, p_ref, wp_ref, o_ref):
    x = x_ref[...]
    h = _rms_rows(x, g_ref[...]).astype(BF16)
    gate = jax.nn.sigmoid(_dot(h, wg_ref[...]))
    ple = _dot(p_ref[...].astype(BF16), wp_ref[...])
    o_ref[...] = x + gate * ple


def _ple(x, g, wg, p, wp, tm):
    t, d = x.shape
    pd = p.shape[1]
    return pl.pallas_call(
        _ple_kernel,
        grid=(t // tm,),
        in_specs=[pl.BlockSpec((tm, d), lambda i: (i, 0)),
                  pl.BlockSpec((1, d), lambda i: (0, 0)),
                  pl.BlockSpec((d, d), lambda i: (0, 0)),
                  pl.BlockSpec((tm, pd), lambda i: (i, 0)),
                  pl.BlockSpec((pd, d), lambda i: (0, 0))],
        out_specs=pl.BlockSpec((tm, d), lambda i: (i, 0)),
        out_shape=jax.ShapeDtypeStruct((t, d), F32),
        compiler_params=_params(("parallel",)),
        name="gated_ple",
    )(x, g.reshape(1, d), wg, p, wp)


def _layer(x, p, attn_norm_g, w_in, sgu_norm_g, w_s, b_s, q_norm_g, k_norm_g,
           w_up_a, w_up_b, w_o, ffn_norm_g, w_ffn_in, w_ffn_out,
           ple_norm_g, w_ple_gate, w_ple, batch, seq):
    t, d = x.shape
    hidden = w_ffn_out.shape[0]
    tm = _tile(t, 1024)
    tm_full = _tile(t, 512)
    tm_big = _tile(t, 2048)
    blk = _tile(seq, 256)
    nsub = next(n for n in (16, 8, 4, 2, 1) if seq % (n * blk) == 0)

    c_qk = 2 * WIDTH
    c_v = c_qk + 2 * WIDTH
    c_gate = c_v + WIDTH

    uv, h = _proj_sgu(x, attn_norm_g, w_in, c_qk, tm_full)
    scale = GROUP ** -0.5
    qk_gain = jnp.concatenate([jnp.tile(q_norm_g * scale, N_GROUPS),
                               jnp.tile(k_norm_g, N_GROUPS)]).reshape(1, 2 * WIDTH)
    qk = _proj(h, w_in, c_qk, 2 * WIDTH, "headnorm", qk_gain, tm, 2048, "proj_qk")
    v_att = _proj(h, w_in, c_v, WIDTH, "none", None, tm_big, 1024, "proj_v")
    gates = _proj(h, w_in, c_gate, 2 * d, "sigmoid", None, tm, 1024, "proj_gates")

    y_b = _attention(qk, v_att, batch, seq, blk, nsub)

    merged = _merge(uv, sgu_norm_g.reshape(-1), w_s, b_s, y_b,
                    w_up_a.astype(BF16), w_up_b.astype(BF16), gates, tm_full)
    x, h = _out_proj(merged, w_o.astype(BF16), x, ffn_norm_g, tm_full)

    act = _ffn_in(h, w_ffn_in, hidden, tm, 512)
    x = _matmul_residual(act, w_ffn_out.astype(BF16), x, tm, 512, "swiglu_out")

    return _ple(x, ple_norm_g, w_ple_gate.astype(BF16), p, w_ple.astype(BF16), tm_full)


def kernel(x, p, attn_norm_g, w_in, sgu_norm_g, w_s, b_s, q_norm_g, k_norm_g, w_up_a, w_up_b, w_o,
           ffn_norm_g, w_ffn_in, w_ffn_out, ple_norm_g, w_ple_gate, w_ple):
    batch, seq, d = x.shape
    xf = x.reshape(batch * seq, d)
    for i in range(p.shape[0]):
        xf = _layer(xf, p[i].reshape(batch * seq, -1), attn_norm_g[i], w_in[i], sgu_norm_g[i], w_s[i],
                    b_s[i], q_norm_g[i], k_norm_g[i], w_up_a[i], w_up_b[i], w_o[i], ffn_norm_g[i],
                    w_ffn_in[i], w_ffn_out[i], ple_norm_g[i], w_ple_gate[i], w_ple[i], batch, seq)
    return xf.reshape(batch, seq, d)
```

```python
import functools

import jax
import jax.numpy as jnp
from jax import lax
from jax.experimental import pallas as pl
from jax.experimental.pallas import tpu as pltpu

F32 = jnp.float32
BF16 = jnp.bfloat16

EPS = 1e-6
GROUP = 128
N_GROUPS = 8
WIDTH = GROUP * N_GROUPS

V7X_VMEM_BYTES = 64 * 1024 * 1024
VMEM_LIMIT = V7X_VMEM_BYTES - 8 * 1024 * 1024

EXP_UNDERFLOW = -104.0


def _tile(n, want):
    if n <= want:
        return n
    t = (want // 128) * 128
    while t >= 128:
        if n % t == 0:
            return t
        t -= 128
    return n


def _params(sem):
    return pltpu.CompilerParams(dimension_semantics=sem, vmem_limit_bytes=VMEM_LIMIT)


def _dot(a, b):
    return jnp.dot(a, b, preferred_element_type=F32)


def _sigmoid(x):
    return 0.5 * jnp.tanh(0.5 * x) + 0.5


def _rms_rows(x, g):
    y = x * lax.rsqrt(jnp.mean(x * x, axis=-1, keepdims=True) + EPS)
    return y * g


def _cast_weight_once(w_ref, wb_ref, axis):
    @pl.when(pl.program_id(axis) == 0)
    def _():
        wb_ref[...] = w_ref[...].astype(wb_ref.dtype)


def _proj_sgu_kernel(x_ref, g_ref, w_ref, o_ref, h_ref, wb_ref):
    _cast_weight_once(w_ref, wb_ref, 0)
    half = x_ref.shape[0] // 2
    for rows in (slice(0, half), slice(half, 2 * half)):
        h = _rms_rows(x_ref[rows, :], g_ref[...]).astype(h_ref.dtype)
        h_ref[rows, :] = h
        o_ref[rows, :] = jax.nn.gelu(_dot(h, wb_ref[...])).astype(o_ref.dtype)


def _proj_sgu(x, g, w, ncols, tm):
    t, d = x.shape
    return pl.pallas_call(
        _proj_sgu_kernel,
        grid=(t // tm,),
        in_specs=[pl.BlockSpec((tm, d), lambda i: (i, 0)),
                  pl.BlockSpec((1, d), lambda i: (0, 0)),
                  pl.BlockSpec((d, ncols), lambda i: (0, 0))],
        out_specs=[pl.BlockSpec((tm, ncols), lambda i: (i, 0)),
                   pl.BlockSpec((tm, d), lambda i: (i, 0))],
        out_shape=[jax.ShapeDtypeStruct((t, ncols), BF16), jax.ShapeDtypeStruct((t, d), BF16)],
        scratch_shapes=[pltpu.VMEM((d, ncols), BF16)],
        compiler_params=_params(("arbitrary",)),
        name="proj_sgu",
    )(x, g.reshape(1, d), w)


def _proj_kernel(h_ref, w_ref, gain_ref, o_ref, wb_ref, *, kind):
    _cast_weight_once(w_ref, wb_ref, 1)
    acc = _dot(h_ref[...], wb_ref[...])
    if kind == "sigmoid":
        o_ref[...] = _sigmoid(acc).astype(o_ref.dtype)
    elif kind == "none":
        o_ref[...] = acc.astype(o_ref.dtype)
    elif kind == "headnorm":
        gain = gain_ref[...]
        for c in range(acc.shape[1] // GROUP):
            sl = slice(c * GROUP, (c + 1) * GROUP)
            o_ref[:, sl] = _rms_rows(acc[:, sl], gain[:, sl]).astype(o_ref.dtype)
    else:
        raise ValueError(kind)


def _proj(h, w, col0, ncols, kind, gain, tm, tn, name):
    t, d = h.shape
    tn = _tile(ncols, tn)
    assert col0 % tn == 0
    off = col0 // tn
    if gain is None:
        gain = jnp.ones((1, ncols), F32)
    return pl.pallas_call(
        functools.partial(_proj_kernel, kind=kind),
        grid=(ncols // tn, t // tm),
        in_specs=[pl.BlockSpec((tm, d), lambda j, i: (i, 0)),
                  pl.BlockSpec((d, tn), lambda j, i: (0, j + off)),
                  pl.BlockSpec((1, tn), lambda j, i: (0, j))],
        out_specs=pl.BlockSpec((tm, tn), lambda j, i: (i, j)),
        out_shape=jax.ShapeDtypeStruct((t, ncols), BF16),
        scratch_shapes=[pltpu.VMEM((d, tn), BF16)],
        compiler_params=_params(("parallel", "arbitrary")),
        name=name,
    )(h, w, gain)


def _attn_kernel(q_ref, k_ref, v_ref, o_ref, *, blk, nsub):
    i0 = pl.program_id(2) * nsub
    row = lax.broadcasted_iota(jnp.int32, (blk, blk), 0)
    col = lax.broadcasted_iota(jnp.int32, (blk, blk), 1)
    strict = col < row
    later = strict.astype(BF16)
    row2 = lax.broadcasted_iota(jnp.int32, (2 * blk, blk), 0)
    col2 = lax.broadcasted_iota(jnp.int32, (2 * blk, blk), 1)
    strict2 = jnp.logical_or(col2 < row2, row2 >= blk)

    def scores(q, j):
        start = pl.multiple_of(j * blk, blk)
        k = k_ref[pl.ds(start, blk), :]
        z = lax.dot_general(q, k, (((1,), (1,)), ((), ())), preferred_element_type=F32)
        log_beta = jnp.minimum(z, 0.0) - jnp.log(1.0 + jnp.exp(-jnp.abs(z)))
        return log_beta, log_beta - z

    def weights(log_beta, log_1m, carry, j):
        start = pl.multiple_of(j * blk, blk)
        v = v_ref[pl.ds(start, blk), :]
        tail = _dot(log_1m.astype(BF16), later) + carry
        return jnp.exp(log_beta + tail), v

    def tile(q, j, carry, mask):
        log_beta, log_1m = scores(q, j)
        if mask is not None:
            log_1m = jnp.where(mask, log_1m, 0.0)
        w, v = weights(log_beta, log_1m, carry, j)
        if mask is not None:
            w = jnp.where(mask, w, 0.0)
        return _dot(w.astype(BF16), v), jnp.sum(log_1m, axis=1, keepdims=True)

    zero_c = jnp.zeros((blk, 1), F32)

    lb, l1 = [], []
    for r in range(nsub - 1):
        log_beta, log_1m = scores(q_ref[r * blk:(r + 2) * blk, :], i0 + r)
        lb.append(log_beta)
        l1.append(jnp.where(strict2, log_1m, 0.0))
    sums = [jnp.sum(x, axis=1, keepdims=True) for x in l1]
    last = nsub - 1
    out_last, sum_last = tile(q_ref[last * blk:(last + 1) * blk, :], i0 + last, zero_c, strict)
    diag_sum = [s[:blk] for s in sums] + [sum_last]
    prev_valid = i0 >= 1
    out_first, sum_first = tile(q_ref[0:blk, :], jnp.maximum(i0 - 1, 0),
                                diag_sum[0] + jnp.where(prev_valid, 0.0, -1e30), None)
    sum_first = jnp.where(prev_valid, sum_first, 0.0)
    outs = []
    for r in range(nsub - 1):
        carry2 = jnp.concatenate([zero_c, diag_sum[r + 1]], axis=0)
        w, v = weights(lb[r], l1[r], carry2, i0 + r)
        w = jnp.where(strict2, w, 0.0)
        outs.append(_dot(w.astype(BF16), v))
    accs, carries = [], []
    for r in range(nsub):
        diag_out = outs[r][:blk] if r < nsub - 1 else out_last
        prev_out = outs[r - 1][blk:] if r >= 1 else out_first
        prev_sum = sums[r - 1][blk:] if r >= 1 else sum_first
        accs.append(diag_out + prev_out)
        carries.append(diag_sum[r] + prev_sum)

    def step(n, carries, accs):
        new_c, new_a = [], []
        for r in range(nsub):
            q = q_ref[r * blk:(r + 1) * blk, :]
            j = i0 + r - n
            valid = j >= 0
            out, rowsum = tile(q, jnp.maximum(j, 0),
                               carries[r] + jnp.where(valid, 0.0, -1e30), None)
            new_a.append(accs[r] + out)
            new_c.append(carries[r] + jnp.where(valid, rowsum, 0.0))
        return tuple(new_c), tuple(new_a)

    def cond(state):
        n, carries, _ = state
        worst = jnp.full((blk, 1), -1e30, F32)
        for r in range(nsub):
            worst = jnp.maximum(worst, jnp.where(i0 + r - n >= 0, carries[r], -1e30))
        return jnp.max(worst) > EXP_UNDERFLOW

    def body(state):
        n, carries, accs = state
        carries, accs = step(n, carries, accs)
        return n + 1, carries, accs

    _, _, accs = lax.while_loop(cond, body, (jnp.int32(2), tuple(carries), tuple(accs)))
    for r in range(nsub):
        o_ref[r * blk:(r + 1) * blk, :] = accs[r].astype(o_ref.dtype)


def _attention(qk, v, batch, seq, blk, nsub):
    t = qk.shape[0]
    rows = blk * nsub
    nq = seq // rows
    return pl.pallas_call(
        functools.partial(_attn_kernel, blk=blk, nsub=nsub),
        grid=(batch, N_GROUPS, nq),
        in_specs=[pl.BlockSpec((rows, GROUP), lambda b, h, i: (b * nq + i, h)),
                  pl.BlockSpec((seq, GROUP), lambda b, h, i: (b, N_GROUPS + h)),
                  pl.BlockSpec((seq, GROUP), lambda b, h, i: (b, h))],
        out_specs=pl.BlockSpec((rows, GROUP), lambda b, h, i: (b * nq + i, h)),
        out_shape=jax.ShapeDtypeStruct((t, WIDTH), BF16),
        compiler_params=_params(("parallel", "parallel", "arbitrary")),
        name="stick_breaking_attention",
    )(qk, qk, v)


def _sgu_tile(u_ref, v_ref, g_ref, ws_ref, b_ref, y_ref, tm):
    nchunk = tm // GROUP
    row = lax.broadcasted_iota(jnp.int32, (GROUP, GROUP), 0)
    col = lax.broadcasted_iota(jnp.int32, (GROUP, GROUP), 1)
    causal = row >= col
    for g in range(N_GROUPS):
        sl = slice(g * GROUP, (g + 1) * GROUP)
        vn = _rms_rows(v_ref[:, sl].astype(F32), g_ref[:, sl]).astype(BF16)
        ws = jnp.where(causal, ws_ref[g], 0.0).astype(BF16)
        vcat = jnp.concatenate([vn[c * GROUP:(c + 1) * GROUP, :] for c in range(nchunk)], axis=1)
        mixed = _dot(ws, vcat) + b_ref[:, g:g + 1]
        for c in range(nchunk):
            rows = slice(c * GROUP, (c + 1) * GROUP)
            u = u_ref[rows, sl].astype(F32)
            y_ref[rows, sl] = (u * mixed[:, c * GROUP:(c + 1) * GROUP]).astype(y_ref.dtype)


def _merge_kernel(u_ref, v_ref, sg_ref, ws_ref, b_ref, yb_ref, wa_ref, wb_ref, ga_ref, gb_ref, o_ref,
                  ya_ref, wab_ref, wbb_ref, *, tm):
    _cast_weight_once(wa_ref, wab_ref, 0)
    _cast_weight_once(wb_ref, wbb_ref, 0)
    _sgu_tile(u_ref, v_ref, sg_ref, ws_ref, b_ref, ya_ref, tm)
    b = _dot(yb_ref[...], wbb_ref[...])
    a = _dot(ya_ref[...], wab_ref[...])
    o_ref[...] = (ga_ref[...].astype(F32) * a + gb_ref[...].astype(F32) * b).astype(o_ref.dtype)


def _merge(uv, sgu_g, w_s, b_s, yb, wa, wb, gates, tm):
    t = yb.shape[0]
    d = wa.shape[1]
    return pl.pallas_call(
        functools.partial(_merge_kernel, tm=tm),
        grid=(t // tm,),
        in_specs=[pl.BlockSpec((tm, WIDTH), lambda i: (i, 0)),
                  pl.BlockSpec((tm, WIDTH), lambda i: (i, 1)),
                  pl.BlockSpec((1, WIDTH), lambda i: (0, 0)),
                  pl.BlockSpec((N_GROUPS, GROUP, GROUP), lambda i: (0, 0, 0)),
                  pl.BlockSpec((GROUP, N_GROUPS), lambda i: (0, 0)),
                  pl.BlockSpec((tm, WIDTH), lambda i: (i, 0)),
                  pl.BlockSpec((WIDTH, d), lambda i: (0, 0)),
                  pl.BlockSpec((WIDTH, d), lambda i: (0, 0)),
                  pl.BlockSpec((tm, d), lambda i: (i, 0)),
                  pl.BlockSpec((tm, d), lambda i: (i, 1))],
        out_specs=pl.BlockSpec((tm, d), lambda i: (i, 0)),
        out_shape=jax.ShapeDtypeStruct((t, d), BF16),
        scratch_shapes=[pltpu.VMEM((tm, WIDTH), BF16), pltpu.VMEM((WIDTH, d), BF16),
                        pltpu.VMEM((WIDTH, d), BF16)],
        compiler_params=_params(("arbitrary",)),
        name="sgu_gated_merge",
    )(uv, uv, sgu_g.reshape(1, WIDTH), w_s, jnp.transpose(b_s), yb, wa, wb, gates, gates)


def _out_proj_kernel(a_ref, w_ref, r_ref, g_ref, x_ref, h_ref, wb_ref):
    _cast_weight_once(w_ref, wb_ref, 0)
    x = r_ref[...] + _dot(a_ref[...], wb_ref[...])
    x_ref[...] = x
    h_ref[...] = _rms_rows(x, g_ref[...]).astype(h_ref.dtype)


def _out_proj(a, w, r, g, tm):
    t, k = a.shape
    d = w.shape[1]
    return pl.pallas_call(
        _out_proj_kernel,
        grid=(t // tm,),
        in_specs=[pl.BlockSpec((tm, k), lambda i: (i, 0)),
                  pl.BlockSpec((k, d), lambda i: (0, 0)),
                  pl.BlockSpec((tm, d), lambda i: (i, 0)),
                  pl.BlockSpec((1, d), lambda i: (0, 0))],
        out_specs=[pl.BlockSpec((tm, d), lambda i: (i, 0)),
                   pl.BlockSpec((tm, d), lambda i: (i, 0))],
        out_shape=[jax.ShapeDtypeStruct((t, d), F32), jax.ShapeDtypeStruct((t, d), BF16)],
        scratch_shapes=[pltpu.VMEM((k, d), BF16)],
        compiler_params=_params(("arbitrary",)),
        name="out_proj",
    )(a, w, r, g.reshape(1, d))


def _ffn_in_kernel(h_ref, wg_ref, wu_ref, o_ref, wgb_ref, wub_ref):
    _cast_weight_once(wg_ref, wgb_ref, 1)
    _cast_weight_once(wu_ref, wub_ref, 1)
    h = h_ref[...]
    gate = _dot(h, wgb_ref[...])
    up = _dot(h, wub_ref[...])
    o_ref[...] = (gate * _sigmoid(gate) * up).astype(o_ref.dtype)


def _ffn_in(h, w, hidden, tm, tn):
    t, d = h.shape
    tn = _tile(hidden, tn)
    nb = hidden // tn
    return pl.pallas_call(
        _ffn_in_kernel,
        grid=(nb, t // tm),
        in_specs=[pl.BlockSpec((tm, d), lambda j, i: (i, 0)),
                  pl.BlockSpec((d, tn), lambda j, i: (0, j)),
                  pl.BlockSpec((d, tn), lambda j, i: (0, j + nb))],
        out_specs=pl.BlockSpec((tm, tn), lambda j, i: (i, j)),
        out_shape=jax.ShapeDtypeStruct((t, hidden), BF16),
        scratch_shapes=[pltpu.VMEM((d, tn), BF16), pltpu.VMEM((d, tn), BF16)],
        compiler_params=_params(("parallel", "arbitrary")),
        name="swiglu_in",
    )(h, w, w)


def _matmul_residual_kernel(a_ref, w_ref, r_ref, o_ref):
    o_ref[...] = r_ref[...] + _dot(a_ref[...], w_ref[...])


def _matmul_residual(a, w, r, tm, tn, name):
    t, k = a.shape
    d = w.shape[1]
    tn = _tile(d, tn)
    return pl.pallas_call(
        _matmul_residual_kernel,
        grid=(t // tm, d // tn),
        in_specs=[pl.BlockSpec((tm, k), lambda i, j: (i, 0)),
                  pl.BlockSpec((k, tn), lambda i, j: (0, j)),
                  pl.BlockSpec((tm, tn), lambda i, j: (i, j))],
        out_specs=pl.BlockSpec((tm, tn), lambda i, j: (i, j)),
        out_shape=jax.ShapeDtypeStruct((t, d), F32),
        compiler_params=_params(("parallel", "parallel")),
        name=name,
    )(a, w, r)


def _ple_kernel(x_ref, g_ref, wg_ref, p_ref, wp_ref, o_ref):
    x = x_ref[...]
    h = _rms_rows(x, g_ref[...]).astype(BF16)
    gate = jax.nn.sigmoid(_dot(h, wg_ref[...]))
    ple = _dot(p_ref[...].astype(BF16), wp_ref[...])
    o_ref[...] = x + gate * ple


def _ple(x, g, wg, p, wp, tm):
    t, d = x.shape
    pd = p.shape[1]
    return pl.pallas_call(
        _ple_kernel,
        grid=(t // tm,),
        in_specs=[pl.BlockSpec((tm, d), lambda i: (i, 0)),
                  pl.BlockSpec((1, d), lambda i: (0, 0)),
                  pl.BlockSpec((d, d), lambda i: (0, 0)),
                  pl.BlockSpec((tm, pd), lambda i: (i, 0)),
                  pl.BlockSpec((pd, d), lambda i: (0, 0))],
        out_specs=pl.BlockSpec((tm, d), lambda i: (i, 0)),
        out_shape=jax.ShapeDtypeStruct((t, d), F32),
        compiler_params=_params(("parallel",)),
        name="gated_ple",
    )(x, g.reshape(1, d), wg, p, wp)


def _layer(x, p, attn_norm_g, w_in, sgu_norm_g, w_s, b_s, q_norm_g, k_norm_g,
           w_up_a, w_up_b, w_o, ffn_norm_g, w_ffn_in, w_ffn_out,
           ple_norm_g, w_ple_gate, w_ple, batch, seq):
    t, d = x.shape
    hidden = w_ffn_out.shape[0]
    tm = _tile(t, 1024)
    tm_full = _tile(t, 512)
    tm_big = _tile(t, 2048)
    blk = _tile(seq, 256)
    nsub = next(n for n in (16, 8, 4, 2, 1) if seq % (n * blk) == 0)

    c_qk = 2 * WIDTH
    c_v = c_qk + 2 * WIDTH
    c_gate = c_v + WIDTH

    uv, h = _proj_sgu(x, attn_norm_g, w_in, c_qk, tm_full)
    scale = GROUP ** -0.5
    qk_gain = jnp.concatenate([jnp.tile(q_norm_g * scale, N_GROUPS),
                               jnp.tile(k_norm_g, N_GROUPS)]).reshape(1, 2 * WIDTH)
    qk = _proj(h, w_in, c_qk, 2 * WIDTH, "headnorm", qk_gain, tm, 2048, "proj_qk")
    v_att = _proj(h, w_in, c_v, WIDTH, "none", None, tm_big, 1024, "proj_v")
    gates = _proj(h, w_in, c_gate, 2 * d, "sigmoid", None, tm, 1024, "proj_gates")

    y_b = _attention(qk, v_att, batch, seq, blk, nsub)

    merged = _merge(uv, sgu_norm_g.reshape(-1), w_s, b_s, y_b,
                    w_up_a, w_up_b, gates, tm_full)
    x, h = _out_proj(merged, w_o, x, ffn_norm_g, tm_full)

    act = _ffn_in(h, w_ffn_in, hidden, tm, 512)
    x = _matmul_residual(act, w_ffn_out.astype(BF16), x, tm, 512, "swiglu_out")

    return _ple(x, ple_norm_g, w_ple_gate.astype(BF16), p, w_ple.astype(BF16), tm_full)


def kernel(x, p, attn_norm_g, w_in, sgu_norm_g, w_s, b_s, q_norm_g, k_norm_g, w_up_a, w_up_b, w_o,
           ffn_norm_g, w_ffn_in, w_ffn_out, ple_norm_g, w_ple_gate, w_ple):
    batch, seq, d = x.shape
    xf = x.reshape(batch * seq, d)
    for i in range(p.shape[0]):
        xf = _layer(xf, p[i].reshape(batch * seq, -1), attn_norm_g[i], w_in[i], sgu_norm_g[i], w_s[i],
                    b_s[i], q_norm_g[i], k_norm_g[i], w_up_a[i], w_up_b[i], w_o[i], ffn_norm_g[i],
                    w_ffn_in[i], w_ffn_out[i], ple_norm_g[i], w_ple_gate[i], w_ple[i], batch, seq)
    return xf.reshape(batch, seq, d)
```

```python
import functools

import jax
import jax.numpy as jnp
from jax import lax
from jax.experimental import pallas as pl
from jax.experimental.pallas import tpu as pltpu

F32 = jnp.float32
BF16 = jnp.bfloat16

EPS = 1e-6
GROUP = 128
N_GROUPS = 8
WIDTH = GROUP * N_GROUPS

V7X_VMEM_BYTES = 64 * 1024 * 1024
VMEM_LIMIT = V7X_VMEM_BYTES - 8 * 1024 * 1024

EXP_UNDERFLOW = -104.0


def _tile(n, want):
    if n <= want:
        return n
    t = (want // 128) * 128
    while t >= 128:
        if n % t == 0:
            return t
        t -= 128
    return n


def _params(sem):
    return pltpu.CompilerParams(dimension_semantics=sem, vmem_limit_bytes=VMEM_LIMIT)


def _dot(a, b):
    return jnp.dot(a, b, preferred_element_type=F32)


def _sigmoid(x):
    return 0.5 * jnp.tanh(0.5 * x) + 0.5


def _rms_rows(x, g):
    y = x * lax.rsqrt(jnp.mean(x * x, axis=-1, keepdims=True) + EPS)
    return y * g


def _cast_weight_once(w_ref, wb_ref, axis):
    @pl.when(pl.program_id(axis) == 0)
    def _():
        wb_ref[...] = w_ref[...].astype(wb_ref.dtype)


def _proj_sgu_kernel(x_ref, g_ref, w_ref, o_ref, h_ref, wb_ref):
    _cast_weight_once(w_ref, wb_ref, 0)
    h = _rms_rows(x_ref[...], g_ref[...]).astype(h_ref.dtype)
    h_ref[...] = h
    o_ref[...] = jax.nn.gelu(_dot(h, wb_ref[...])).astype(o_ref.dtype)


def _proj_sgu(x, g, w, ncols, tm):
    t, d = x.shape
    return pl.pallas_call(
        _proj_sgu_kernel,
        grid=(t // tm,),
        in_specs=[pl.BlockSpec((tm, d), lambda i: (i, 0)),
                  pl.BlockSpec((1, d), lambda i: (0, 0)),
                  pl.BlockSpec((d, ncols), lambda i: (0, 0))],
        out_specs=[pl.BlockSpec((tm, ncols), lambda i: (i, 0)),
                   pl.BlockSpec((tm, d), lambda i: (i, 0))],
        out_shape=[jax.ShapeDtypeStruct((t, ncols), BF16), jax.ShapeDtypeStruct((t, d), BF16)],
        scratch_shapes=[pltpu.VMEM((d, ncols), BF16)],
        compiler_params=_params(("arbitrary",)),
        name="proj_sgu",
    )(x, g.reshape(1, d), w)


def _proj_kernel(h_ref, w_ref, gain_ref, o_ref, wb_ref, *, kind):
    _cast_weight_once(w_ref, wb_ref, 1)
    acc = _dot(h_ref[...], wb_ref[...])
    if kind == "sigmoid":
        o_ref[...] = _sigmoid(acc).astype(o_ref.dtype)
    elif kind == "none":
        o_ref[...] = acc.astype(o_ref.dtype)
    elif kind == "headnorm":
        gain = gain_ref[...]
        for c in range(acc.shape[1] // GROUP):
            sl = slice(c * GROUP, (c + 1) * GROUP)
            o_ref[:, sl] = _rms_rows(acc[:, sl], gain[:, sl]).astype(o_ref.dtype)
    else:
        raise ValueError(kind)


def _proj(h, w, col0, ncols, kind, gain, tm, tn, name):
    t, d = h.shape
    tn = _tile(ncols, tn)
    assert col0 % tn == 0
    off = col0 // tn
    if gain is None:
        gain = jnp.ones((1, ncols), F32)
    return pl.pallas_call(
        functools.partial(_proj_kernel, kind=kind),
        grid=(ncols // tn, t // tm),
        in_specs=[pl.BlockSpec((tm, d), lambda j, i: (i, 0)),
                  pl.BlockSpec((d, tn), lambda j, i: (0, j + off)),
                  pl.BlockSpec((1, tn), lambda j, i: (0, j))],
        out_specs=pl.BlockSpec((tm, tn), lambda j, i: (i, j)),
        out_shape=jax.ShapeDtypeStruct((t, ncols), BF16),
        scratch_shapes=[pltpu.VMEM((d, tn), BF16)],
        compiler_params=_params(("parallel", "arbitrary")),
        name=name,
    )(h, w, gain)


def _attn_kernel(q_ref, k_ref, v_ref, o_ref, *, blk, nsub):
    i0 = pl.program_id(2) * nsub
    row = lax.broadcasted_iota(jnp.int32, (blk, blk), 0)
    col = lax.broadcasted_iota(jnp.int32, (blk, blk), 1)
    strict = col < row
    later = strict.astype(BF16)
    row2 = lax.broadcasted_iota(jnp.int32, (2 * blk, blk), 0)
    col2 = lax.broadcasted_iota(jnp.int32, (2 * blk, blk), 1)
    strict2 = jnp.logical_or(col2 < row2, row2 >= blk)

    def scores(q, j):
        start = pl.multiple_of(j * blk, blk)
        k = k_ref[pl.ds(start, blk), :]
        z = lax.dot_general(q, k, (((1,), (1,)), ((), ())), preferred_element_type=F32)
        log_beta = jnp.minimum(z, 0.0) - jnp.log(1.0 + jnp.exp(-jnp.abs(z)))
        return log_beta, log_beta - z

    def weights(log_beta, log_1m, carry, j):
        start = pl.multiple_of(j * blk, blk)
        v = v_ref[pl.ds(start, blk), :]
        tail = _dot(log_1m.astype(BF16), later) + carry
        return jnp.exp(log_beta + tail), v

    def tile(q, j, carry, mask):
        log_beta, log_1m = scores(q, j)
        if mask is not None:
            log_1m = jnp.where(mask, log_1m, 0.0)
        w, v = weights(log_beta, log_1m, carry, j)
        if mask is not None:
            w = jnp.where(mask, w, 0.0)
        return _dot(w.astype(BF16), v), jnp.sum(log_1m, axis=1, keepdims=True)

    zero_c = jnp.zeros((blk, 1), F32)

    lb, l1 = [], []
    for r in range(nsub - 1):
        log_beta, log_1m = scores(q_ref[r * blk:(r + 2) * blk, :], i0 + r)
        lb.append(log_beta)
        l1.append(jnp.where(strict2, log_1m, 0.0))
    sums = [jnp.sum(x, axis=1, keepdims=True) for x in l1]
    last = nsub - 1
    out_last, sum_last = tile(q_ref[last * blk:(last + 1) * blk, :], i0 + last, zero_c, strict)
    diag_sum = [s[:blk] for s in sums] + [sum_last]
    prev_valid = i0 >= 1
    out_first, sum_first = tile(q_ref[0:blk, :], jnp.maximum(i0 - 1, 0),
                                diag_sum[0] + jnp.where(prev_valid, 0.0, -1e30), None)
    sum_first = jnp.where(prev_valid, sum_first, 0.0)
    outs = []
    for r in range(nsub - 1):
        carry2 = jnp.concatenate([zero_c, diag_sum[r + 1]], axis=0)
        w, v = weights(lb[r], l1[r], carry2, i0 + r)
        w = jnp.where(strict2, w, 0.0)
        outs.append(_dot(w.astype(BF16), v))
    accs, carries = [], []
    for r in range(nsub):
        diag_out = outs[r][:blk] if r < nsub - 1 else out_last
        prev_out = outs[r - 1][blk:] if r >= 1 else out_first
        prev_sum = sums[r - 1][blk:] if r >= 1 else sum_first
        accs.append(diag_out + prev_out)
        carries.append(diag_sum[r] + prev_sum)

    def step(n, carries, accs):
        new_c, new_a = [], []
        for r in range(nsub):
            q = q_ref[r * blk:(r + 1) * blk, :]
            j = i0 + r - n
            valid = j >= 0
            out, rowsum = tile(q, jnp.maximum(j, 0),
                               carries[r] + jnp.where(valid, 0.0, -1e30), None)
            new_a.append(accs[r] + out)
            new_c.append(carries[r] + jnp.where(valid, rowsum, 0.0))
        return tuple(new_c), tuple(new_a)

    def cond(state):
        n, carries, _ = state
        worst = jnp.full((blk, 1), -1e30, F32)
        for r in range(nsub):
            worst = jnp.maximum(worst, jnp.where(i0 + r - n >= 0, carries[r], -1e30))
        return jnp.max(worst) > EXP_UNDERFLOW

    def body(state):
        n, carries, accs = state
        carries, accs = step(n, carries, accs)
        return n + 1, carries, accs

    _, _, accs = lax.while_loop(cond, body, (jnp.int32(2), tuple(carries), tuple(accs)))
    for r in range(nsub):
        o_ref[r * blk:(r + 1) * blk, :] = accs[r].astype(o_ref.dtype)


def _attention(qk, v, batch, seq, blk, nsub):
    t = qk.shape[0]
    rows = blk * nsub
    nq = seq // rows
    return pl.pallas_call(
        functools.partial(_attn_kernel, blk=blk, nsub=nsub),
        grid=(batch, N_GROUPS, nq),
        in_specs=[pl.BlockSpec((rows, GROUP), lambda b, h, i: (b * nq + i, h)),
                  pl.BlockSpec((seq, GROUP), lambda b, h, i: (b, N_GROUPS + h)),
                  pl.BlockSpec((seq, GROUP), lambda b, h, i: (b, h))],
        out_specs=pl.BlockSpec((rows, GROUP), lambda b, h, i: (b * nq + i, h)),
        out_shape=jax.ShapeDtypeStruct((t, WIDTH), BF16),
        compiler_params=_params(("parallel", "parallel", "arbitrary")),
        name="stick_breaking_attention",
    )(qk, qk, v)


def _sgu_tile(u_ref, v_ref, g_ref, ws_ref, b_ref, y_ref, tm):
    nchunk = tm // GROUP
    row = lax.broadcasted_iota(jnp.int32, (GROUP, GROUP), 0)
    col = lax.broadcasted_iota(jnp.int32, (GROUP, GROUP), 1)
    causal = row >= col
    for g in range(N_GROUPS):
        sl = slice(g * GROUP, (g + 1) * GROUP)
        vn = _rms_rows(v_ref[:, sl].astype(F32), g_ref[:, sl]).astype(BF16)
        ws = jnp.where(causal, ws_ref[g], 0.0).astype(BF16)
        vcat = jnp.concatenate([vn[c * GROUP:(c + 1) * GROUP, :] for c in range(nchunk)], axis=1)
        mixed = _dot(ws, vcat) + b_ref[:, g:g + 1]
        for c in range(nchunk):
            rows = slice(c * GROUP, (c + 1) * GROUP)
            u = u_ref[rows, sl].astype(F32)
            y_ref[rows, sl] = (u * mixed[:, c * GROUP:(c + 1) * GROUP]).astype(y_ref.dtype)


def _merge_kernel(u_ref, v_ref, sg_ref, ws_ref, b_ref, yb_ref, wa_ref, wb_ref, ga_ref, gb_ref, o_ref,
                  ya_ref, *, tm):
    _sgu_tile(u_ref, v_ref, sg_ref, ws_ref, b_ref, ya_ref, tm)
    b = _dot(yb_ref[...], wb_ref[...])
    a = _dot(ya_ref[...], wa_ref[...])
    o_ref[...] = (ga_ref[...].astype(F32) * a + gb_ref[...].astype(F32) * b).astype(o_ref.dtype)


def _merge(uv, sgu_g, w_s, b_s, yb, wa, wb, gates, tm):
    t = yb.shape[0]
    d = wa.shape[1]
    return pl.pallas_call(
        functools.partial(_merge_kernel, tm=tm),
        grid=(t // tm,),
        in_specs=[pl.BlockSpec((tm, WIDTH), lambda i: (i, 0)),
                  pl.BlockSpec((tm, WIDTH), lambda i: (i, 1)),
                  pl.BlockSpec((1, WIDTH), lambda i: (0, 0)),
                  pl.BlockSpec((N_GROUPS, GROUP, GROUP), lambda i: (0, 0, 0)),
                  pl.BlockSpec((GROUP, N_GROUPS), lambda i: (0, 0)),
                  pl.BlockSpec((tm, WIDTH), lambda i: (i, 0)),
                  pl.BlockSpec((WIDTH, d), lambda i: (0, 0)),
                  pl.BlockSpec((WIDTH, d), lambda i: (0, 0)),
                  pl.BlockSpec((tm, d), lambda i: (i, 0)),
                  pl.BlockSpec((tm, d), lambda i: (i, 1))],
        out_specs=pl.BlockSpec((tm, d), lambda i: (i, 0)),
        out_shape=jax.ShapeDtypeStruct((t, d), BF16),
        scratch_shapes=[pltpu.VMEM((tm, WIDTH), BF16)],
        compiler_params=_params(("parallel",)),
        name="sgu_gated_merge",
    )(uv, uv, sgu_g.reshape(1, WIDTH), w_s, jnp.transpose(b_s), yb, wa, wb, gates, gates)


def _out_proj_kernel(a_ref, w_ref, r_ref, g_ref, x_ref, h_ref):
    x = r_ref[...] + _dot(a_ref[...], w_ref[...])
    x_ref[...] = x
    h_ref[...] = _rms_rows(x, g_ref[...]).astype(h_ref.dtype)


def _out_proj(a, w, r, g, tm):
    t, k = a.shape
    d = w.shape[1]
    return pl.pallas_call(
        _out_proj_kernel,
        grid=(t // tm,),
        in_specs=[pl.BlockSpec((tm, k), lambda i: (i, 0)),
                  pl.BlockSpec((k, d), lambda i: (0, 0)),
                  pl.BlockSpec((tm, d), lambda i: (i, 0)),
                  pl.BlockSpec((1, d), lambda i: (0, 0))],
        out_specs=[pl.BlockSpec((tm, d), lambda i: (i, 0)),
                   pl.BlockSpec((tm, d), lambda i: (i, 0))],
        out_shape=[jax.ShapeDtypeStruct((t, d), F32), jax.ShapeDtypeStruct((t, d), BF16)],
        compiler_params=_params(("parallel",)),
        name="out_proj",
    )(a, w, r, g.reshape(1, d))


def _ffn_in_kernel(h_ref, wg_ref, wu_ref, o_ref, wgb_ref, wub_ref):
    _cast_weight_once(wg_ref, wgb_ref, 1)
    _cast_weight_once(wu_ref, wub_ref, 1)
    h = h_ref[...]
    gate = _dot(h, wgb_ref[...])
    up = _dot(h, wub_ref[...])
    o_ref[...] = (gate * _sigmoid(gate) * up).astype(o_ref.dtype)


def _ffn_in(h, w, hidden, tm, tn):
    t, d = h.shape
    tn = _tile(hidden, tn)
    nb = hidden // tn
    return pl.pallas_call(
        _ffn_in_kernel,
        grid=(nb, t // tm),
        in_specs=[pl.BlockSpec((tm, d), lambda j, i: (i, 0)),
                  pl.BlockSpec((d, tn), lambda j, i: (0, j)),
                  pl.BlockSpec((d, tn), lambda j, i: (0, j + nb))],
        out_specs=pl.BlockSpec((tm, tn), lambda j, i: (i, j)),
        out_shape=jax.ShapeDtypeStruct((t, hidden), BF16),
        scratch_shapes=[pltpu.VMEM((d, tn), BF16), pltpu.VMEM((d, tn), BF16)],
        compiler_params=_params(("parallel", "arbitrary")),
        name="swiglu_in",
    )(h, w, w)


def _matmul_residual_kernel(a_hbm, w_ref, r_ref, o_ref, a_buf, a_sem, *, tm):
    i = pl.program_id(0)
    j = pl.program_id(1)

    def fetch(row_step):
        slot = row_step % 2
        return pltpu.make_async_copy(a_hbm.at[pl.ds(row_step * tm, tm), :], a_buf.at[slot],
                                     a_sem.at[slot])

    @pl.when(jnp.logical_and(i == 0, j == 0))
    def _():
        fetch(0).start()

    @pl.when(j == 0)
    def _():
        @pl.when(i + 1 < pl.num_programs(0))
        def _():
            fetch(i + 1).start()
        fetch(i).wait()

    o_ref[...] = r_ref[...] + _dot(a_buf[i % 2], w_ref[...])


def _matmul_residual(a, w, r, tm, tn, name):
    t, k = a.shape
    d = w.shape[1]
    tn = _tile(d, tn)
    return pl.pallas_call(
        functools.partial(_matmul_residual_kernel, tm=tm),
        grid=(t // tm, d // tn),
        in_specs=[pl.BlockSpec(memory_space=pl.ANY),
                  pl.BlockSpec((k, tn), lambda i, j: (0, j)),
                  pl.BlockSpec((tm, tn), lambda i, j: (i, j))],
        out_specs=pl.BlockSpec((tm, tn), lambda i, j: (i, j)),
        out_shape=jax.ShapeDtypeStruct((t, d), F32),
        scratch_shapes=[pltpu.VMEM((2, tm, k), a.dtype), pltpu.SemaphoreType.DMA((2,))],
        compiler_params=_params(("arbitrary", "arbitrary")),
        name=name,
    )(a, w, r)


def _ple_kernel(x_ref, g_ref, wg_ref, p_ref, wp_ref, o_ref):
    x = x_ref[...]
    h = _rms_rows(x, g_ref[...]).astype(BF16)
    gate = jax.nn.sigmoid(_dot(h, wg_ref[...]))
    ple = _dot(p_ref[...].astype(BF16), wp_ref[...])
    o_ref[...] = x + gate * ple


def _ple(x, g, wg, p, wp, tm):
    t, d = x.shape
    pd = p.shape[1]
    return pl.pallas_call(
        _ple_kernel,
        grid=(t // tm,),
        in_specs=[pl.BlockSpec((tm, d), lambda i: (i, 0)),
                  pl.BlockSpec((1, d), lambda i: (0, 0)),
                  pl.BlockSpec((d, d), lambda i: (0, 0)),
                  pl.BlockSpec((tm, pd), lambda i: (i, 0)),
                  pl.BlockSpec((pd, d), lambda i: (0, 0))],
        out_specs=pl.BlockSpec((tm, d), lambda i: (i, 0)),
        out_shape=jax.ShapeDtypeStruct((t, d), F32),
        compiler_params=_params(("parallel",)),
        name="gated_ple",
    )(x, g.reshape(1, d), wg, p, wp)


def _layer(x, p, attn_norm_g, w_in, sgu_norm_g, w_s, b_s, q_norm_g, k_norm_g,
           w_up_a, w_up_b, w_o, ffn_norm_g, w_ffn_in, w_ffn_out,
           ple_norm_g, w_ple_gate, w_ple, batch, seq):
    t, d = x.shape
    hidden = w_ffn_out.shape[0]
    tm = _tile(t, 1024)
    tm_full = _tile(t, 512)
    tm_big = _tile(t, 2048)
    blk = _tile(seq, 256)
    nsub = next(n for n in (16, 8, 4, 2, 1) if seq % (n * blk) == 0)

    c_qk = 2 * WIDTH
    c_v = c_qk + 2 * WIDTH
    c_gate = c_v + WIDTH

    uv, h = _proj_sgu(x, attn_norm_g, w_in, c_qk, tm_full)
    scale = GROUP ** -0.5
    qk_gain = jnp.concatenate([jnp.tile(q_norm_g * scale, N_GROUPS),
                               jnp.tile(k_norm_g, N_GROUPS)]).reshape(1, 2 * WIDTH)
    qk = _proj(h, w_in, c_qk, 2 * WIDTH, "headnorm", qk_gain, tm, 2048, "proj_qk")
    v_att = _proj(h, w_in, c_v, WIDTH, "none", None, tm_big, 1024, "proj_v")
    gates = _proj(h, w_in, c_gate, 2 * d, "sigmoid", None, tm, 1024, "proj_gates")

    y_b = _attention(qk, v_att, batch, seq, blk, nsub)

    merged = _merge(uv, sgu_norm_g.reshape(-1), w_s, b_s, y_b,
                    w_up_a.astype(BF16), w_up_b.astype(BF16), gates, tm_full)
    x, h = _out_proj(merged, w_o.astype(BF16), x, ffn_norm_g, tm_full)

    act = _ffn_in(h, w_ffn_in, hidden, tm, 512)
    x = _matmul_residual(act, w_ffn_out.astype(BF16), x, tm, 512, "swiglu_out")

    return _ple(x, ple_norm_g, w_ple_gate.astype(BF16), p, w_ple.astype(BF16), tm_full)


def kernel(x, p, attn_norm_g, w_in, sgu_norm_g, w_s, b_s, q_norm_g, k_norm_g, w_up_a, w_up_b, w_o,
           ffn_norm_g, w_ffn_in, w_ffn_out, ple_norm_g, w_ple_gate, w_ple):
    batch, seq, d = x.shape
    xf = x.reshape(batch * seq, d)
    for i in range(p.shape[0]):
        xf = _layer(xf, p[i].reshape(batch * seq, -1), attn_norm_g[i], w_in[i], sgu_norm_g[i], w_s[i],
                    b_s[i], q_norm_g[i], k_norm_g[i], w_up_a[i], w_up_b[i], w_o[i], ffn_norm_g[i],
                    w_ffn_in[i], w_ffn_out[i], ple_norm_g[i], w_ple_gate[i], w_ple[i], batch, seq)
    return xf.reshape(batch, seq, d)
```

```python
import functools

import jax
import jax.numpy as jnp
from jax import lax
from jax.experimental import pallas as pl
from jax.experimental.pallas import tpu as pltpu

F32 = jnp.float32
BF16 = jnp.bfloat16

EPS = 1e-6
GROUP = 128
N_GROUPS = 8
WIDTH = GROUP * N_GROUPS

V7X_VMEM_BYTES = 64 * 1024 * 1024
VMEM_LIMIT = V7X_VMEM_BYTES - 8 * 1024 * 1024

LOG2_E = 1.4426950408889634
GELU_C1 = 0.7978845608028654
GELU_C2 = GELU_C1 * 0.044715
EXP_UNDERFLOW = -150.0


def _tile(n, want):
    if n <= want:
        return n
    t = (want // 128) * 128
    while t >= 128:
        if n % t == 0:
            return t
        t -= 128
    return n


def _params(sem):
    return pltpu.CompilerParams(dimension_semantics=sem, vmem_limit_bytes=VMEM_LIMIT)


def _dot(a, b):
    return jnp.dot(a, b, preferred_element_type=F32)


def _sigmoid(x):
    return 0.5 * jnp.tanh(0.5 * x) + 0.5


def _gelu(x):
    return x * (0.5 * jnp.tanh(x * (GELU_C1 + GELU_C2 * (x * x))) + 0.5)


def _rms_rows(x, g):
    y = x * lax.rsqrt(jnp.mean(x * x, axis=-1, keepdims=True) + EPS)
    return y * g


def _cast_weight_once(w_ref, wb_ref, axis):
    @pl.when(pl.program_id(axis) == 0)
    def _():
        wb_ref[...] = w_ref[...].astype(wb_ref.dtype)


def _proj_sgu_kernel(x_ref, g_ref, w_ref, o_ref, h_ref, wb_ref):
    _cast_weight_once(w_ref, wb_ref, 0)
    h = _rms_rows(x_ref[...], g_ref[...]).astype(h_ref.dtype)
    h_ref[...] = h
    o_ref[...] = _gelu(_dot(h, wb_ref[...])).astype(o_ref.dtype)


def _proj_sgu(x, g, w, ncols, tm):
    t, d = x.shape
    return pl.pallas_call(
        _proj_sgu_kernel,
        grid=(t // tm,),
        in_specs=[pl.BlockSpec((tm, d), lambda i: (i, 0)),
                  pl.BlockSpec((1, d), lambda i: (0, 0)),
                  pl.BlockSpec((d, ncols), lambda i: (0, 0))],
        out_specs=[pl.BlockSpec((tm, ncols), lambda i: (i, 0)),
                   pl.BlockSpec((tm, d), lambda i: (i, 0))],
        out_shape=[jax.ShapeDtypeStruct((t, ncols), BF16), jax.ShapeDtypeStruct((t, d), BF16)],
        scratch_shapes=[pltpu.VMEM((d, ncols), BF16)],
        compiler_params=_params(("arbitrary",)),
        name="proj_sgu",
    )(x, g.reshape(1, d), w)


def _proj_kernel(h_ref, w_ref, gain_ref, o_ref, wb_ref, *, kind):
    _cast_weight_once(w_ref, wb_ref, 1)
    acc = _dot(h_ref[...], wb_ref[...])
    if kind == "sigmoid":
        o_ref[...] = _sigmoid(acc).astype(o_ref.dtype)
    elif kind == "none":
        o_ref[...] = acc.astype(o_ref.dtype)
    elif kind == "headnorm":
        gain = gain_ref[...]
        for c in range(acc.shape[1] // GROUP):
            sl = slice(c * GROUP, (c + 1) * GROUP)
            o_ref[:, sl] = _rms_rows(acc[:, sl], gain[:, sl]).astype(o_ref.dtype)
    else:
        raise ValueError(kind)


def _proj(h, w, col0, ncols, kind, gain, tm, tn, name):
    t, d = h.shape
    tn = _tile(ncols, tn)
    assert col0 % tn == 0
    off = col0 // tn
    if gain is None:
        gain = jnp.ones((1, ncols), F32)
    return pl.pallas_call(
        functools.partial(_proj_kernel, kind=kind),
        grid=(ncols // tn, t // tm),
        in_specs=[pl.BlockSpec((tm, d), lambda j, i: (i, 0)),
                  pl.BlockSpec((d, tn), lambda j, i: (0, j + off)),
                  pl.BlockSpec((1, tn), lambda j, i: (0, j))],
        out_specs=pl.BlockSpec((tm, tn), lambda j, i: (i, j)),
        out_shape=jax.ShapeDtypeStruct((t, ncols), BF16),
        scratch_shapes=[pltpu.VMEM((d, tn), BF16)],
        compiler_params=_params(("parallel", "arbitrary")),
        name=name,
    )(h, w, gain)


def _attn_kernel(q_ref, k_ref, v_ref, o_ref, *, blk, nsub):
    i0 = pl.program_id(2) * nsub
    row = lax.broadcasted_iota(jnp.int32, (blk, blk), 0)
    col = lax.broadcasted_iota(jnp.int32, (blk, blk), 1)
    strict = col < row
    later = strict.astype(BF16)
    row2 = lax.broadcasted_iota(jnp.int32, (2 * blk, blk), 0)
    col2 = lax.broadcasted_iota(jnp.int32, (2 * blk, blk), 1)
    strict2 = jnp.logical_or(col2 < row2, row2 >= blk)

    def scores(q, j):
        start = pl.multiple_of(j * blk, blk)
        k = k_ref[pl.ds(start, blk), :]
        z = lax.dot_general(q, k, (((1,), (1,)), ((), ())), preferred_element_type=F32)
        log_beta = jnp.minimum(z, 0.0) - LOG2_E * jnp.log(1.0 + jnp.exp2(-jnp.abs(z)))
        return log_beta, log_beta - z

    def weights(log_beta, log_1m, carry, j):
        start = pl.multiple_of(j * blk, blk)
        v = v_ref[pl.ds(start, blk), :]
        tail = _dot(log_1m.astype(BF16), later) + carry
        return jnp.exp2(log_beta + tail), v

    def tile(q, j, carry, mask):
        log_beta, log_1m = scores(q, j)
        if mask is not None:
            log_1m = jnp.where(mask, log_1m, 0.0)
        w, v = weights(log_beta, log_1m, carry, j)
        if mask is not None:
            w = jnp.where(mask, w, 0.0)
        return _dot(w.astype(BF16), v), jnp.sum(log_1m, axis=1, keepdims=True)

    zero_c = jnp.zeros((blk, 1), F32)

    lb, l1 = [], []
    for r in range(nsub - 1):
        log_beta, log_1m = scores(q_ref[r * blk:(r + 2) * blk, :], i0 + r)
        lb.append(log_beta)
        l1.append(jnp.where(strict2, log_1m, 0.0))
    sums = [jnp.sum(x, axis=1, keepdims=True) for x in l1]
    last = nsub - 1
    out_last, sum_last = tile(q_ref[last * blk:(last + 1) * blk, :], i0 + last, zero_c, strict)
    diag_sum = [s[:blk] for s in sums] + [sum_last]
    prev_valid = i0 >= 1
    out_first, sum_first = tile(q_ref[0:blk, :], jnp.maximum(i0 - 1, 0),
                                diag_sum[0] + jnp.where(prev_valid, 0.0, -1e30), None)
    sum_first = jnp.where(prev_valid, sum_first, 0.0)
    outs = []
    for r in range(nsub - 1):
        carry2 = jnp.concatenate([zero_c, diag_sum[r + 1]], axis=0)
        w, v = weights(lb[r], l1[r], carry2, i0 + r)
        w = jnp.where(strict2, w, 0.0)
        outs.append(_dot(w.astype(BF16), v))
    accs, carries = [], []
    for r in range(nsub):
        diag_out = outs[r][:blk] if r < nsub - 1 else out_last
        prev_out = outs[r - 1][blk:] if r >= 1 else out_first
        prev_sum = sums[r - 1][blk:] if r >= 1 else sum_first
        accs.append(diag_out + prev_out)
        carries.append(diag_sum[r] + prev_sum)

    def step(n, carries, accs):
        new_c, new_a = [], []
        for r in range(nsub):
            q = q_ref[r * blk:(r + 1) * blk, :]
            j = i0 + r - n
            valid = j >= 0
            out, rowsum = tile(q, jnp.maximum(j, 0),
                               carries[r] + jnp.where(valid, 0.0, -1e30), None)
            new_a.append(accs[r] + out)
            new_c.append(carries[r] + jnp.where(valid, rowsum, 0.0))
        return tuple(new_c), tuple(new_a)

    def cond(state):
        n, carries, _ = state
        worst = jnp.full((blk, 1), -1e30, F32)
        for r in range(nsub):
            worst = jnp.maximum(worst, jnp.where(i0 + r - n >= 0, carries[r], -1e30))
        return jnp.max(worst) > EXP_UNDERFLOW

    def body(state):
        n, carries, accs = state
        carries, accs = step(n, carries, accs)
        return n + 1, carries, accs

    _, _, accs = lax.while_loop(cond, body, (jnp.int32(2), tuple(carries), tuple(accs)))
    for r in range(nsub):
        o_ref[r * blk:(r + 1) * blk, :] = accs[r].astype(o_ref.dtype)


def _attention(qk, v, batch, seq, blk, nsub):
    t = qk.shape[0]
    rows = blk * nsub
    nq = seq // rows
    return pl.pallas_call(
        functools.partial(_attn_kernel, blk=blk, nsub=nsub),
        grid=(batch, N_GROUPS, nq),
        in_specs=[pl.BlockSpec((rows, GROUP), lambda b, h, i: (b * nq + i, h)),
                  pl.BlockSpec((seq, GROUP), lambda b, h, i: (b, N_GROUPS + h)),
                  pl.BlockSpec((seq, GROUP), lambda b, h, i: (b, h))],
        out_specs=pl.BlockSpec((rows, GROUP), lambda b, h, i: (b * nq + i, h)),
        out_shape=jax.ShapeDtypeStruct((t, WIDTH), BF16),
        compiler_params=_params(("parallel", "parallel", "arbitrary")),
        name="stick_breaking_attention",
    )(qk, qk, v)


def _sgu_tile(u_ref, v_ref, g_ref, ws_ref, b_ref, y_ref, tm):
    nchunk = tm // GROUP
    row = lax.broadcasted_iota(jnp.int32, (GROUP, GROUP), 0)
    col = lax.broadcasted_iota(jnp.int32, (GROUP, GROUP), 1)
    causal = row >= col
    for g in range(N_GROUPS):
        sl = slice(g * GROUP, (g + 1) * GROUP)
        vn = _rms_rows(v_ref[:, sl].astype(F32), g_ref[:, sl]).astype(BF16)
        ws = jnp.where(causal, ws_ref[g], 0.0).astype(BF16)
        vcat = jnp.concatenate([vn[c * GROUP:(c + 1) * GROUP, :] for c in range(nchunk)], axis=1)
        mixed = _dot(ws, vcat) + b_ref[:, g:g + 1]
        for c in range(nchunk):
            rows = slice(c * GROUP, (c + 1) * GROUP)
            u = u_ref[rows, sl].astype(F32)
            y_ref[rows, sl] = (u * mixed[:, c * GROUP:(c + 1) * GROUP]).astype(y_ref.dtype)


def _merge_kernel(u_ref, v_ref, sg_ref, ws_ref, b_ref, yb_ref, wa_ref, wb_ref, ga_ref, gb_ref, o_ref,
                  ya_ref, *, tm):
    _sgu_tile(u_ref, v_ref, sg_ref, ws_ref, b_ref, ya_ref, tm)
    b = _dot(yb_ref[...], wb_ref[...])
    a = _dot(ya_ref[...], wa_ref[...])
    o_ref[...] = (ga_ref[...].astype(F32) * a + gb_ref[...].astype(F32) * b).astype(o_ref.dtype)


def _merge(uv, sgu_g, w_s, b_s, yb, wa, wb, gates, tm):
    t = yb.shape[0]
    d = wa.shape[1]
    return pl.pallas_call(
        functools.partial(_merge_kernel, tm=tm),
        grid=(t // tm,),
        in_specs=[pl.BlockSpec((tm, WIDTH), lambda i: (i, 0)),
                  pl.BlockSpec((tm, WIDTH), lambda i: (i, 1)),
                  pl.BlockSpec((1, WIDTH), lambda i: (0, 0)),
                  pl.BlockSpec((N_GROUPS, GROUP, GROUP), lambda i: (0, 0, 0)),
                  pl.BlockSpec((GROUP, N_GROUPS), lambda i: (0, 0)),
                  pl.BlockSpec((tm, WIDTH), lambda i: (i, 0)),
                  pl.BlockSpec((WIDTH, d), lambda i: (0, 0)),
                  pl.BlockSpec((WIDTH, d), lambda i: (0, 0)),
                  pl.BlockSpec((tm, d), lambda i: (i, 0)),
                  pl.BlockSpec((tm, d), lambda i: (i, 1))],
        out_specs=pl.BlockSpec((tm, d), lambda i: (i, 0)),
        out_shape=jax.ShapeDtypeStruct((t, d), BF16),
        scratch_shapes=[pltpu.VMEM((tm, WIDTH), BF16)],
        compiler_params=_params(("parallel",)),
        name="sgu_gated_merge",
    )(uv, uv, sgu_g.reshape(1, WIDTH), w_s, jnp.transpose(b_s), yb, wa, wb, gates, gates)


def _out_proj_kernel(a_ref, w_ref, r_ref, g_ref, x_ref, h_ref):
    x = r_ref[...] + _dot(a_ref[...], w_ref[...])
    x_ref[...] = x
    h_ref[...] = _rms_rows(x, g_ref[...]).astype(h_ref.dtype)


def _out_proj(a, w, r, g, tm):
    t, k = a.shape
    d = w.shape[1]
    return pl.pallas_call(
        _out_proj_kernel,
        grid=(t // tm,),
        in_specs=[pl.BlockSpec((tm, k), lambda i: (i, 0)),
                  pl.BlockSpec((k, d), lambda i: (0, 0)),
                  pl.BlockSpec((tm, d), lambda i: (i, 0)),
                  pl.BlockSpec((1, d), lambda i: (0, 0))],
        out_specs=[pl.BlockSpec((tm, d), lambda i: (i, 0)),
                   pl.BlockSpec((tm, d), lambda i: (i, 0))],
        out_shape=[jax.ShapeDtypeStruct((t, d), F32), jax.ShapeDtypeStruct((t, d), BF16)],
        compiler_params=_params(("parallel",)),
        name="out_proj",
    )(a, w, r, g.reshape(1, d))


def _ffn_in_kernel(h_ref, wg_ref, wu_ref, o_ref, wgb_ref, wub_ref):
    _cast_weight_once(wg_ref, wgb_ref, 1)
    _cast_weight_once(wu_ref, wub_ref, 1)
    h = h_ref[...]
    gate = _dot(h, wgb_ref[...])
    up = _dot(h, wub_ref[...])
    o_ref[...] = (gate * _sigmoid(gate) * up).astype(o_ref.dtype)


def _ffn_in(h, w, hidden, tm, tn):
    t, d = h.shape
    tn = _tile(hidden, tn)
    nb = hidden // tn
    return pl.pallas_call(
        _ffn_in_kernel,
        grid=(nb, t // tm),
        in_specs=[pl.BlockSpec((tm, d), lambda j, i: (i, 0)),
                  pl.BlockSpec((d, tn), lambda j, i: (0, j)),
                  pl.BlockSpec((d, tn), lambda j, i: (0, j + nb))],
        out_specs=pl.BlockSpec((tm, tn), lambda j, i: (i, j)),
        out_shape=jax.ShapeDtypeStruct((t, hidden), BF16),
        scratch_shapes=[pltpu.VMEM((d, tn), BF16), pltpu.VMEM((d, tn), BF16)],
        compiler_params=_params(("parallel", "arbitrary")),
        name="swiglu_in",
    )(h, w, w)


def _matmul_residual_kernel(a_hbm, w_ref, r_ref, o_ref, a_buf, a_sem, *, tm):
    i = pl.program_id(0)
    j = pl.program_id(1)

    def fetch(row_step):
        slot = row_step % 2
        return pltpu.make_async_copy(a_hbm.at[pl.ds(row_step * tm, tm), :], a_buf.at[slot],
                                     a_sem.at[slot])

    @pl.when(jnp.logical_and(i == 0, j == 0))
    def _():
        fetch(0).start()

    @pl.when(j == 0)
    def _():
        @pl.when(i + 1 < pl.num_programs(0))
        def _():
            fetch(i + 1).start()
        fetch(i).wait()

    o_ref[...] = r_ref[...] + _dot(a_buf[i % 2], w_ref[...])


def _matmul_residual(a, w, r, tm, tn, name):
    t, k = a.shape
    d = w.shape[1]
    tn = _tile(d, tn)
    return pl.pallas_call(
        functools.partial(_matmul_residual_kernel, tm=tm),
        grid=(t // tm, d // tn),
        in_specs=[pl.BlockSpec(memory_space=pl.ANY),
                  pl.BlockSpec((k, tn), lambda i, j: (0, j)),
                  pl.BlockSpec((tm, tn), lambda i, j: (i, j))],
        out_specs=pl.BlockSpec((tm, tn), lambda i, j: (i, j)),
        out_shape=jax.ShapeDtypeStruct((t, d), F32),
        scratch_shapes=[pltpu.VMEM((2, tm, k), a.dtype), pltpu.SemaphoreType.DMA((2,))],
        compiler_params=_params(("arbitrary", "arbitrary")),
        name=name,
    )(a, w, r)


def _ple_kernel(x_ref, g_ref, wg_ref, p_ref, wp_ref, o_ref):
    x = x_ref[...]
    r = lax.rsqrt(jnp.mean(x * x, axis=-1, keepdims=True) + EPS)
    gate = jax.nn.sigmoid(r * _dot((x * g_ref[...]).astype(BF16), wg_ref[...]))
    ple = _dot(p_ref[...].astype(BF16), wp_ref[...])
    o_ref[...] = x + gate * ple


def _ple(x, g, wg, p, wp, tm):
    t, d = x.shape
    pd = p.shape[1]
    return pl.pallas_call(
        _ple_kernel,
        grid=(t // tm,),
        in_specs=[pl.BlockSpec((tm, d), lambda i: (i, 0)),
                  pl.BlockSpec((1, d), lambda i: (0, 0)),
                  pl.BlockSpec((d, d), lambda i: (0, 0)),
                  pl.BlockSpec((tm, pd), lambda i: (i, 0)),
                  pl.BlockSpec((pd, d), lambda i: (0, 0))],
        out_specs=pl.BlockSpec((tm, d), lambda i: (i, 0)),
        out_shape=jax.ShapeDtypeStruct((t, d), F32),
        compiler_params=_params(("parallel",)),
        name="gated_ple",
    )(x, g.reshape(1, d), wg, p, wp)


def _layer(x, p, attn_norm_g, w_in, sgu_norm_g, w_s, b_s, q_norm_g, k_norm_g,
           w_up_a, w_up_b, w_o, ffn_norm_g, w_ffn_in, w_ffn_out,
           ple_norm_g, w_ple_gate, w_ple, batch, seq):
    t, d = x.shape
    hidden = w_ffn_out.shape[0]
    tm = _tile(t, 1024)
    tm_full = _tile(t, 512)
    tm_big = _tile(t, 2048)
    blk = _tile(seq, 256)
    nsub = next(n for n in (16, 8, 4, 2, 1) if seq % (n * blk) == 0)

    c_qk = 2 * WIDTH
    c_v = c_qk + 2 * WIDTH
    c_gate = c_v + WIDTH

    uv, h = _proj_sgu(x, attn_norm_g, w_in, c_qk, tm_full)
    scale = GROUP ** -0.5 * LOG2_E
    qk_gain = jnp.concatenate([jnp.tile(q_norm_g * scale, N_GROUPS),
                               jnp.tile(k_norm_g, N_GROUPS)]).reshape(1, 2 * WIDTH)
    qk = _proj(h, w_in, c_qk, 2 * WIDTH, "headnorm", qk_gain, tm, 2048, "proj_qk")
    v_att = _proj(h, w_in, c_v, WIDTH, "none", None, tm_big, 1024, "proj_v")
    gates = _proj(h, w_in, c_gate, 2 * d, "sigmoid", None, tm, 1024, "proj_gates")

    y_b = _attention(qk, v_att, batch, seq, blk, nsub)

    merged = _merge(uv, sgu_norm_g.reshape(-1), w_s, b_s, y_b,
                    w_up_a.astype(BF16), w_up_b.astype(BF16), gates, tm_full)
    x, h = _out_proj(merged, w_o.astype(BF16), x, ffn_norm_g, tm_full)

    act = _ffn_in(h, w_ffn_in, hidden, tm, 512)
    x = _matmul_residual(act, w_ffn_out.astype(BF16), x, tm, 512, "swiglu_out")

    return _ple(x, ple_norm_g, w_ple_gate.astype(BF16), p, w_ple.astype(BF16), tm_full)


def kernel(x, p, attn_norm_g, w_in, sgu_norm_g, w_s, b_s, q_norm_g, k_norm_g, w_up_a, w_up_b, w_o,
           ffn_norm_g, w_ffn_in, w_ffn_out, ple_norm_g, w_ple_gate, w_ple):
    batch, seq, d = x.shape
    xf = x.reshape(batch * seq, d)
    for i in range(p.shape[0]):
        xf = _layer(xf, p[i].reshape(batch * seq, -1), attn_norm_g[i], w_in[i], sgu_norm_g[i], w_s[i],
                    b_s[i], q_norm_g[i], k_norm_g[i], w_up_a[i], w_up_b[i], w_o[i], ffn_norm_g[i],
                    w_ffn_in[i], w_ffn_out[i], ple_norm_g[i], w_ple_gate[i], w_ple[i], batch, seq)
    return xf.reshape(batch, seq, d)
```

```python
import functools

import jax
import jax.numpy as jnp
from jax import lax
from jax.experimental import pallas as pl
from jax.experimental.pallas import tpu as pltpu

F32 = jnp.float32
BF16 = jnp.bfloat16

EPS = 1e-6
GROUP = 128
N_GROUPS = 8
WIDTH = GROUP * N_GROUPS

V7X_VMEM_BYTES = 64 * 1024 * 1024
VMEM_LIMIT = V7X_VMEM_BYTES - 8 * 1024 * 1024

LOG2_E = 1.4426950408889634
GELU_C1 = 0.7978845608028654
GELU_C2 = GELU_C1 * 0.044715
EXP_UNDERFLOW = -150.0


def _tile(n, want):
    if n <= want:
        return n
    t = (want // 128) * 128
    while t >= 128:
        if n % t == 0:
            return t
        t -= 128
    return n


def _params(sem):
    return pltpu.CompilerParams(dimension_semantics=sem, vmem_limit_bytes=VMEM_LIMIT)


def _dot(a, b):
    return jnp.dot(a, b, preferred_element_type=F32)


def _sigmoid(x):
    return 0.5 * jnp.tanh(0.5 * x) + 0.5


def _gelu(x):
    return x * (0.5 * jnp.tanh(x * (GELU_C1 + GELU_C2 * (x * x))) + 0.5)


def _rms_rows(x, g):
    y = x * lax.rsqrt(jnp.mean(x * x, axis=-1, keepdims=True) + EPS)
    return y * g


def _cast_weight_once(w_ref, wb_ref, axis):
    @pl.when(pl.program_id(axis) == 0)
    def _():
        wb_ref[...] = w_ref[...].astype(wb_ref.dtype)


def _proj_sgu_kernel(x_ref, g_ref, w_ref, o_ref, h_ref, wb_ref):
    _cast_weight_once(w_ref, wb_ref, 0)
    h = _rms_rows(x_ref[...], g_ref[...]).astype(h_ref.dtype)
    h_ref[...] = h
    o_ref[...] = _gelu(_dot(h, wb_ref[...])).astype(o_ref.dtype)


def _proj_sgu(x, g, w, ncols, tm):
    t, d = x.shape
    return pl.pallas_call(
        _proj_sgu_kernel,
        grid=(t // tm,),
        in_specs=[pl.BlockSpec((tm, d), lambda i: (i, 0)),
                  pl.BlockSpec((1, d), lambda i: (0, 0)),
                  pl.BlockSpec((d, ncols), lambda i: (0, 0))],
        out_specs=[pl.BlockSpec((tm, ncols), lambda i: (i, 0)),
                   pl.BlockSpec((tm, d), lambda i: (i, 0))],
        out_shape=[jax.ShapeDtypeStruct((t, ncols), BF16), jax.ShapeDtypeStruct((t, d), BF16)],
        scratch_shapes=[pltpu.VMEM((d, ncols), BF16)],
        compiler_params=_params(("arbitrary",)),
        name="proj_sgu",
    )(x, g.reshape(1, d), w)


def _proj_kernel(h_ref, w_ref, gain_ref, o_ref, wb_ref, *, kind):
    _cast_weight_once(w_ref, wb_ref, 1)
    if kind == "sigmoid":
        half = h_ref.shape[0] // 2
        for rows in (slice(0, half), slice(half, 2 * half)):
            o_ref[rows, :] = _sigmoid(_dot(h_ref[rows, :], wb_ref[...])).astype(o_ref.dtype)
        return
    acc = _dot(h_ref[...], wb_ref[...])
    if kind == "none":
        o_ref[...] = acc.astype(o_ref.dtype)
    elif kind == "headnorm":
        gain = gain_ref[...]
        for c in range(acc.shape[1] // GROUP):
            sl = slice(c * GROUP, (c + 1) * GROUP)
            o_ref[:, sl] = _rms_rows(acc[:, sl], gain[:, sl]).astype(o_ref.dtype)
    else:
        raise ValueError(kind)


def _proj(h, w, col0, ncols, kind, gain, tm, tn, name):
    t, d = h.shape
    tn = _tile(ncols, tn)
    assert col0 % tn == 0
    off = col0 // tn
    if gain is None:
        gain = jnp.ones((1, ncols), F32)
    return pl.pallas_call(
        functools.partial(_proj_kernel, kind=kind),
        grid=(ncols // tn, t // tm),
        in_specs=[pl.BlockSpec((tm, d), lambda j, i: (i, 0)),
                  pl.BlockSpec((d, tn), lambda j, i: (0, j + off)),
                  pl.BlockSpec((1, tn), lambda j, i: (0, j))],
        out_specs=pl.BlockSpec((tm, tn), lambda j, i: (i, j)),
        out_shape=jax.ShapeDtypeStruct((t, ncols), BF16),
        scratch_shapes=[pltpu.VMEM((d, tn), BF16)],
        compiler_params=_params(("parallel", "arbitrary")),
        name=name,
    )(h, w, gain)


def _attn_kernel(q_ref, k_ref, v_ref, o_ref, *, blk, nsub):
    i0 = pl.program_id(2) * nsub
    row = lax.broadcasted_iota(jnp.int32, (blk, blk), 0)
    col = lax.broadcasted_iota(jnp.int32, (blk, blk), 1)
    strict = col < row
    later = strict.astype(BF16)
    row2 = lax.broadcasted_iota(jnp.int32, (2 * blk, blk), 0)
    col2 = lax.broadcasted_iota(jnp.int32, (2 * blk, blk), 1)
    strict2 = jnp.logical_or(col2 < row2, row2 >= blk)

    def scores(q, j):
        start = pl.multiple_of(j * blk, blk)
        k = k_ref[pl.ds(start, blk), :]
        z = lax.dot_general(q, k, (((1,), (1,)), ((), ())), preferred_element_type=F32)
        log_beta = jnp.minimum(z, 0.0) - LOG2_E * jnp.log(1.0 + jnp.exp2(-jnp.abs(z)))
        return log_beta, log_beta - z

    def weights(log_beta, log_1m, carry, j):
        start = pl.multiple_of(j * blk, blk)
        v = v_ref[pl.ds(start, blk), :]
        tail = _dot(log_1m.astype(BF16), later) + carry
        return jnp.exp2(log_beta + tail), v

    def tile(q, j, carry, mask):
        log_beta, log_1m = scores(q, j)
        if mask is not None:
            log_1m = jnp.where(mask, log_1m, 0.0)
        w, v = weights(log_beta, log_1m, carry, j)
        if mask is not None:
            w = jnp.where(mask, w, 0.0)
        return _dot(w.astype(BF16), v), jnp.sum(log_1m, axis=1, keepdims=True)

    zero_c = jnp.zeros((blk, 1), F32)

    lb, l1 = [], []
    for r in range(nsub - 1):
        log_beta, log_1m = scores(q_ref[r * blk:(r + 2) * blk, :], i0 + r)
        lb.append(log_beta)
        l1.append(jnp.where(strict2, log_1m, 0.0))
    sums = [jnp.sum(x, axis=1, keepdims=True) for x in l1]
    last = nsub - 1
    out_last, sum_last = tile(q_ref[last * blk:(last + 1) * blk, :], i0 + last, zero_c, strict)
    diag_sum = [s[:blk] for s in sums] + [sum_last]
    prev_valid = i0 >= 1
    out_first, sum_first = tile(q_ref[0:blk, :], jnp.maximum(i0 - 1, 0),
                                diag_sum[0] + jnp.where(prev_valid, 0.0, -1e30), None)
    sum_first = jnp.where(prev_valid, sum_first, 0.0)
    outs = []
    for r in range(nsub - 1):
        carry2 = jnp.concatenate([zero_c, diag_sum[r + 1]], axis=0)
        w, v = weights(lb[r], l1[r], carry2, i0 + r)
        w = jnp.where(strict2, w, 0.0)
        outs.append(_dot(w.astype(BF16), v))
    accs, carries = [], []
    for r in range(nsub):
        diag_out = outs[r][:blk] if r < nsub - 1 else out_last
        prev_out = outs[r - 1][blk:] if r >= 1 else out_first
        prev_sum = sums[r - 1][blk:] if r >= 1 else sum_first
        accs.append(diag_out + prev_out)
        carries.append(diag_sum[r] + prev_sum)

    def step(n, carries, accs):
        new_c, new_a = [], []
        for r in range(nsub):
            q = q_ref[r * blk:(r + 1) * blk, :]
            j = i0 + r - n
            valid = j >= 0
            out, rowsum = tile(q, jnp.maximum(j, 0),
                               carries[r] + jnp.where(valid, 0.0, -1e30), None)
            new_a.append(accs[r] + out)
            new_c.append(carries[r] + jnp.where(valid, rowsum, 0.0))
        return tuple(new_c), tuple(new_a)

    def cond(state):
        n, carries, _ = state
        worst = jnp.full((blk, 1), -1e30, F32)
        for r in range(nsub):
            worst = jnp.maximum(worst, jnp.where(i0 + r - n >= 0, carries[r], -1e30))
        return jnp.max(worst) > EXP_UNDERFLOW

    def body(state):
        n, carries, accs = state
        carries, accs = step(n, carries, accs)
        return n + 1, carries, accs

    _, _, accs = lax.while_loop(cond, body, (jnp.int32(2), tuple(carries), tuple(accs)))
    for r in range(nsub):
        o_ref[r * blk:(r + 1) * blk, :] = accs[r].astype(o_ref.dtype)


def _attention(qk, v, batch, seq, blk, nsub):
    t = qk.shape[0]
    rows = blk * nsub
    nq = seq // rows
    return pl.pallas_call(
        functools.partial(_attn_kernel, blk=blk, nsub=nsub),
        grid=(batch, N_GROUPS, nq),
        in_specs=[pl.BlockSpec((rows, GROUP), lambda b, h, i: (b * nq + i, h)),
                  pl.BlockSpec((seq, GROUP), lambda b, h, i: (b, N_GROUPS + h)),
                  pl.BlockSpec((seq, GROUP), lambda b, h, i: (b, h))],
        out_specs=pl.BlockSpec((rows, GROUP), lambda b, h, i: (b * nq + i, h)),
        out_shape=jax.ShapeDtypeStruct((t, WIDTH), BF16),
        compiler_params=_params(("parallel", "parallel", "arbitrary")),
        name="stick_breaking_attention",
    )(qk, qk, v)


def _sgu_tile(u_ref, v_ref, g_ref, ws_ref, b_ref, y_ref, tm):
    nchunk = tm // GROUP
    row = lax.broadcasted_iota(jnp.int32, (GROUP, GROUP), 0)
    col = lax.broadcasted_iota(jnp.int32, (GROUP, GROUP), 1)
    causal = row >= col
    for g in range(N_GROUPS):
        sl = slice(g * GROUP, (g + 1) * GROUP)
        vn = _rms_rows(v_ref[:, sl].astype(F32), g_ref[:, sl]).astype(BF16)
        ws = jnp.where(causal, ws_ref[g], 0.0).astype(BF16)
        vcat = jnp.concatenate([vn[c * GROUP:(c + 1) * GROUP, :] for c in range(nchunk)], axis=1)
        mixed = _dot(ws, vcat) + b_ref[:, g:g + 1]
        for c in range(nchunk):
            rows = slice(c * GROUP, (c + 1) * GROUP)
            u = u_ref[rows, sl].astype(F32)
            y_ref[rows, sl] = (u * mixed[:, c * GROUP:(c + 1) * GROUP]).astype(y_ref.dtype)


def _merge_kernel(u_ref, v_ref, sg_ref, ws_ref, b_ref, yb_ref, wa_ref, wb_ref, ga_ref, gb_ref, o_ref,
                  ya_ref, *, tm):
    _sgu_tile(u_ref, v_ref, sg_ref, ws_ref, b_ref, ya_ref, tm)
    b = _dot(yb_ref[...], wb_ref[...])
    a = _dot(ya_ref[...], wa_ref[...])
    o_ref[...] = (ga_ref[...].astype(F32) * a + gb_ref[...].astype(F32) * b).astype(o_ref.dtype)


def _merge(uv, sgu_g, w_s, b_s, yb, wa, wb, gates, tm):
    t = yb.shape[0]
    d = wa.shape[1]
    return pl.pallas_call(
        functools.partial(_merge_kernel, tm=tm),
        grid=(t // tm,),
        in_specs=[pl.BlockSpec((tm, WIDTH), lambda i: (i, 0)),
                  pl.BlockSpec((tm, WIDTH), lambda i: (i, 1)),
                  pl.BlockSpec((1, WIDTH), lambda i: (0, 0)),
                  pl.BlockSpec((N_GROUPS, GROUP, GROUP), lambda i: (0, 0, 0)),
                  pl.BlockSpec((GROUP, N_GROUPS), lambda i: (0, 0)),
                  pl.BlockSpec((tm, WIDTH), lambda i: (i, 0)),
                  pl.BlockSpec((WIDTH, d), lambda i: (0, 0)),
                  pl.BlockSpec((WIDTH, d), lambda i: (0, 0)),
                  pl.BlockSpec((tm, d), lambda i: (i, 0)),
                  pl.BlockSpec((tm, d), lambda i: (i, 1))],
        out_specs=pl.BlockSpec((tm, d), lambda i: (i, 0)),
        out_shape=jax.ShapeDtypeStruct((t, d), BF16),
        scratch_shapes=[pltpu.VMEM((tm, WIDTH), BF16)],
        compiler_params=_params(("parallel",)),
        name="sgu_gated_merge",
    )(uv, uv, sgu_g.reshape(1, WIDTH), w_s, jnp.transpose(b_s), yb, wa, wb, gates, gates)


def _out_proj_kernel(a_ref, w_ref, r_ref, g_ref, x_ref, h_ref):
    x = r_ref[...] + _dot(a_ref[...], w_ref[...])
    x_ref[...] = x
    h_ref[...] = _rms_rows(x, g_ref[...]).astype(h_ref.dtype)


def _out_proj(a, w, r, g, tm):
    t, k = a.shape
    d = w.shape[1]
    return pl.pallas_call(
        _out_proj_kernel,
        grid=(t // tm,),
        in_specs=[pl.BlockSpec((tm, k), lambda i: (i, 0)),
                  pl.BlockSpec((k, d), lambda i: (0, 0)),
                  pl.BlockSpec((tm, d), lambda i: (i, 0)),
                  pl.BlockSpec((1, d), lambda i: (0, 0))],
        out_specs=[pl.BlockSpec((tm, d), lambda i: (i, 0)),
                   pl.BlockSpec((tm, d), lambda i: (i, 0))],
        out_shape=[jax.ShapeDtypeStruct((t, d), F32), jax.ShapeDtypeStruct((t, d), BF16)],
        compiler_params=_params(("parallel",)),
        name="out_proj",
    )(a, w, r, g.reshape(1, d))


def _ffn_in_kernel(h_ref, wg_ref, wu_ref, o_ref, wgb_ref, wub_ref):
    _cast_weight_once(wg_ref, wgb_ref, 1)
    _cast_weight_once(wu_ref, wub_ref, 1)
    half = h_ref.shape[0] // 2
    for rows in (slice(0, half), slice(half, 2 * half)):
        h = h_ref[rows, :]
        gate = _dot(h, wgb_ref[...])
        up = _dot(h, wub_ref[...])
        o_ref[rows, :] = (gate * _sigmoid(gate) * up).astype(o_ref.dtype)


def _ffn_in(h, w, hidden, tm, tn):
    t, d = h.shape
    tn = _tile(hidden, tn)
    nb = hidden // tn
    return pl.pallas_call(
        _ffn_in_kernel,
        grid=(nb, t // tm),
        in_specs=[pl.BlockSpec((tm, d), lambda j, i: (i, 0)),
                  pl.BlockSpec((d, tn), lambda j, i: (0, j)),
                  pl.BlockSpec((d, tn), lambda j, i: (0, j + nb))],
        out_specs=pl.BlockSpec((tm, tn), lambda j, i: (i, j)),
        out_shape=jax.ShapeDtypeStruct((t, hidden), BF16),
        scratch_shapes=[pltpu.VMEM((d, tn), BF16), pltpu.VMEM((d, tn), BF16)],
        compiler_params=_params(("parallel", "arbitrary")),
        name="swiglu_in",
    )(h, w, w)


def _matmul_residual_kernel(a_hbm, w_ref, r_ref, o_ref, a_buf, a_sem, *, tm):
    i = pl.program_id(0)
    j = pl.program_id(1)

    def fetch(row_step):
        slot = row_step % 2
        return pltpu.make_async_copy(a_hbm.at[pl.ds(row_step * tm, tm), :], a_buf.at[slot],
                                     a_sem.at[slot])

    @pl.when(jnp.logical_and(i == 0, j == 0))
    def _():
        fetch(0).start()

    @pl.when(j == 0)
    def _():
        @pl.when(i + 1 < pl.num_programs(0))
        def _():
            fetch(i + 1).start()
        fetch(i).wait()

    o_ref[...] = r_ref[...] + _dot(a_buf[i % 2], w_ref[...])


def _matmul_residual(a, w, r, tm, tn, name):
    t, k = a.shape
    d = w.shape[1]
    tn = _tile(d, tn)
    return pl.pallas_call(
        functools.partial(_matmul_residual_kernel, tm=tm),
        grid=(t // tm, d // tn),
        in_specs=[pl.BlockSpec(memory_space=pl.ANY),
                  pl.BlockSpec((k, tn), lambda i, j: (0, j)),
                  pl.BlockSpec((tm, tn), lambda i, j: (i, j))],
        out_specs=pl.BlockSpec((tm, tn), lambda i, j: (i, j)),
        out_shape=jax.ShapeDtypeStruct((t, d), F32),
        scratch_shapes=[pltpu.VMEM((2, tm, k), a.dtype), pltpu.SemaphoreType.DMA((2,))],
        compiler_params=_params(("arbitrary", "arbitrary")),
        name=name,
    )(a, w, r)


def _ple_kernel(x_ref, g_ref, wg_ref, p_ref, wp_ref, o_ref):
    x = x_ref[...]
    r = lax.rsqrt(jnp.mean(x * x, axis=-1, keepdims=True) + EPS)
    gate = jax.nn.sigmoid(r * _dot((x * g_ref[...]).astype(BF16), wg_ref[...]))
    ple = _dot(p_ref[...].astype(BF16), wp_ref[...])
    o_ref[...] = x + gate * ple


def _ple(x, g, wg, p, wp, tm):
    t, d = x.shape
    pd = p.shape[1]
    return pl.pallas_call(
        _ple_kernel,
        grid=(t // tm,),
        in_specs=[pl.BlockSpec((tm, d), lambda i: (i, 0)),
                  pl.BlockSpec((1, d), lambda i: (0, 0)),
                  pl.BlockSpec((d, d), lambda i: (0, 0)),
                  pl.BlockSpec((tm, pd), lambda i: (i, 0)),
                  pl.BlockSpec((pd, d), lambda i: (0, 0))],
        out_specs=pl.BlockSpec((tm, d), lambda i: (i, 0)),
        out_shape=jax.ShapeDtypeStruct((t, d), F32),
        compiler_params=_params(("parallel",)),
        name="gated_ple",
    )(x, g.reshape(1, d), wg, p, wp)


def _layer(x, p, attn_norm_g, w_in, sgu_norm_g, w_s, b_s, q_norm_g, k_norm_g,
           w_up_a, w_up_b, w_o, ffn_norm_g, w_ffn_in, w_ffn_out,
           ple_norm_g, w_ple_gate, w_ple, batch, seq):
    t, d = x.shape
    hidden = w_ffn_out.shape[0]
    tm = _tile(t, 1024)
    tm_full = _tile(t, 512)
    tm_big = _tile(t, 2048)
    blk = _tile(seq, 256)
    nsub = next(n for n in (16, 8, 4, 2, 1) if seq % (n * blk) == 0)

    c_qk = 2 * WIDTH
    c_v = c_qk + 2 * WIDTH
    c_gate = c_v + WIDTH

    uv, h = _proj_sgu(x, attn_norm_g, w_in, c_qk, tm_full)
    scale = GROUP ** -0.5 * LOG2_E
    qk_gain = jnp.concatenate([jnp.tile(q_norm_g * scale, N_GROUPS),
                               jnp.tile(k_norm_g, N_GROUPS)]).reshape(1, 2 * WIDTH)
    qk = _proj(h, w_in, c_qk, 2 * WIDTH, "headnorm", qk_gain, tm, 2048, "proj_qk")
    v_att = _proj(h, w_in, c_v, WIDTH, "none", None, tm_big, 1024, "proj_v")
    gates = _proj(h, w_in, c_gate, 2 * d, "sigmoid", None, tm_big, 1024, "proj_gates")

    y_b = _attention(qk, v_att, batch, seq, blk, nsub)

    merged = _merge(uv, sgu_norm_g.reshape(-1), w_s, b_s, y_b,
                    w_up_a.astype(BF16), w_up_b.astype(BF16), gates, tm_full)
    x, h = _out_proj(merged, w_o.astype(BF16), x, ffn_norm_g, tm_full)

    act = _ffn_in(h, w_ffn_in, hidden, tm_big, 512)
    x = _matmul_residual(act, w_ffn_out.astype(BF16), x, tm, 512, "swiglu_out")

    return _ple(x, ple_norm_g, w_ple_gate.astype(BF16), p, w_ple.astype(BF16), tm_full)


def kernel(x, p, attn_norm_g, w_in, sgu_norm_g, w_s, b_s, q_norm_g, k_norm_g, w_up_a, w_up_b, w_o,
           ffn_norm_g, w_ffn_in, w_ffn_out, ple_norm_g, w_ple_gate, w_ple):
    batch, seq, d = x.shape
    xf = x.reshape(batch * seq, d)
    for i in range(p.shape[0]):
        xf = _layer(xf, p[i].reshape(batch * seq, -1), attn_norm_g[i], w_in[i], sgu_norm_g[i], w_s[i],
                    b_s[i], q_norm_g[i], k_norm_g[i], w_up_a[i], w_up_b[i], w_o[i], ffn_norm_g[i],
                    w_ffn_in[i], w_ffn_out[i], ple_norm_g[i], w_ple_gate[i], w_ple[i], batch, seq)
    return xf.reshape(batch, seq, d)
```

```python
import functools

import jax
import jax.numpy as jnp
from jax import lax
from jax.experimental import pallas as pl
from jax.experimental.pallas import tpu as pltpu

F32 = jnp.float32
BF16 = jnp.bfloat16

EPS = 1e-6
GROUP = 128
N_GROUPS = 8
WIDTH = GROUP * N_GROUPS

V7X_VMEM_BYTES = 64 * 1024 * 1024
VMEM_LIMIT = V7X_VMEM_BYTES - 8 * 1024 * 1024

LOG2_E = 1.4426950408889634
GELU_C1 = 0.7978845608028654
GELU_C2 = GELU_C1 * 0.044715
EXP_UNDERFLOW = -150.0


def _tile(n, want):
    if n <= want:
        return n
    t = (want // 128) * 128
    while t >= 128:
        if n % t == 0:
            return t
        t -= 128
    return n


def _params(sem):
    return pltpu.CompilerParams(dimension_semantics=sem, vmem_limit_bytes=VMEM_LIMIT)


def _dot(a, b):
    return jnp.dot(a, b, preferred_element_type=F32)


def _sigmoid(x):
    return 0.5 * jnp.tanh(0.5 * x) + 0.5


def _gelu(x):
    return x * (0.5 * jnp.tanh(x * (GELU_C1 + GELU_C2 * (x * x))) + 0.5)


def _rms_rows(x, g):
    y = x * lax.rsqrt(jnp.mean(x * x, axis=-1, keepdims=True) + EPS)
    return y * g


def _cast_weight_once(w_ref, wb_ref, axis):
    @pl.when(pl.program_id(axis) == 0)
    def _():
        wb_ref[...] = w_ref[...].astype(wb_ref.dtype)


def _proj_sgu_kernel(x_ref, g_ref, w_ref, o_ref, h_ref, wb_ref):
    _cast_weight_once(w_ref, wb_ref, 0)
    h = _rms_rows(x_ref[...], g_ref[...]).astype(h_ref.dtype)
    h_ref[...] = h
    o_ref[...] = _gelu(_dot(h, wb_ref[...])).astype(o_ref.dtype)


def _proj_sgu(x, g, w, ncols, tm):
    t, d = x.shape
    return pl.pallas_call(
        _proj_sgu_kernel,
        grid=(t // tm,),
        in_specs=[pl.BlockSpec((tm, d), lambda i: (i, 0)),
                  pl.BlockSpec((1, d), lambda i: (0, 0)),
                  pl.BlockSpec((d, ncols), lambda i: (0, 0))],
        out_specs=[pl.BlockSpec((tm, ncols), lambda i: (i, 0)),
                   pl.BlockSpec((tm, d), lambda i: (i, 0))],
        out_shape=[jax.ShapeDtypeStruct((t, ncols), BF16), jax.ShapeDtypeStruct((t, d), BF16)],
        scratch_shapes=[pltpu.VMEM((d, ncols), BF16)],
        compiler_params=_params(("arbitrary",)),
        name="proj_sgu",
    )(x, g.reshape(1, d), w)


def _proj_kernel(h_ref, w_ref, gain_ref, o_ref, wb_ref, *, kind):
    _cast_weight_once(w_ref, wb_ref, 1)
    if kind == "sigmoid":
        half = h_ref.shape[0] // 2
        for rows in (slice(0, half), slice(half, 2 * half)):
            o_ref[rows, :] = _sigmoid(_dot(h_ref[rows, :], wb_ref[...])).astype(o_ref.dtype)
        return
    acc = _dot(h_ref[...], wb_ref[...])
    if kind == "none":
        o_ref[...] = acc.astype(o_ref.dtype)
    elif kind == "headnorm":
        gain = gain_ref[...]
        for c in range(acc.shape[1] // GROUP):
            sl = slice(c * GROUP, (c + 1) * GROUP)
            o_ref[:, sl] = _rms_rows(acc[:, sl], gain[:, sl]).astype(o_ref.dtype)
    else:
        raise ValueError(kind)


def _proj(h, w, col0, ncols, kind, gain, tm, tn, name):
    t, d = h.shape
    tn = _tile(ncols, tn)
    assert col0 % tn == 0
    off = col0 // tn
    if gain is None:
        gain = jnp.ones((1, ncols), F32)
    return pl.pallas_call(
        functools.partial(_proj_kernel, kind=kind),
        grid=(ncols // tn, t // tm),
        in_specs=[pl.BlockSpec((tm, d), lambda j, i: (i, 0)),
                  pl.BlockSpec((d, tn), lambda j, i: (0, j + off)),
                  pl.BlockSpec((1, tn), lambda j, i: (0, j))],
        out_specs=pl.BlockSpec((tm, tn), lambda j, i: (i, j)),
        out_shape=jax.ShapeDtypeStruct((t, ncols), BF16),
        scratch_shapes=[pltpu.VMEM((d, tn), BF16)],
        compiler_params=_params(("parallel", "arbitrary")),
        name=name,
    )(h, w, gain)


def _attn_kernel(q_ref, k_ref, v_ref, o_ref, *, blk, nsub):
    i0 = pl.program_id(2) * nsub
    row = lax.broadcasted_iota(jnp.int32, (blk, blk), 0)
    col = lax.broadcasted_iota(jnp.int32, (blk, blk), 1)
    strict = col < row
    later = strict.astype(BF16)
    row2 = lax.broadcasted_iota(jnp.int32, (2 * blk, blk), 0)
    col2 = lax.broadcasted_iota(jnp.int32, (2 * blk, blk), 1)
    strict2 = jnp.logical_or(col2 < row2, row2 >= blk)

    def scores(q, j):
        start = pl.multiple_of(j * blk, blk)
        k = k_ref[pl.ds(start, blk), :]
        z = lax.dot_general(q, k, (((1,), (1,)), ((), ())), preferred_element_type=F32)
        log_beta = jnp.minimum(z, 0.0) - LOG2_E * jnp.log(1.0 + jnp.exp2(-jnp.abs(z)))
        return log_beta, log_beta - z

    def weights(log_beta, log_1m, carry, j):
        start = pl.multiple_of(j * blk, blk)
        v = v_ref[pl.ds(start, blk), :]
        tail = _dot(log_1m.astype(BF16), later) + carry
        return jnp.exp2(log_beta + tail), v

    def tile(q, j, carry, mask):
        log_beta, log_1m = scores(q, j)
        if mask is not None:
            log_1m = jnp.where(mask, log_1m, 0.0)
        w, v = weights(log_beta, log_1m, carry, j)
        if mask is not None:
            w = jnp.where(mask, w, 0.0)
        return _dot(w.astype(BF16), v), jnp.sum(log_1m, axis=1, keepdims=True)

    zero_c = jnp.zeros((blk, 1), F32)

    lb, l1 = [], []
    for r in range(nsub - 1):
        log_beta, log_1m = scores(q_ref[r * blk:(r + 2) * blk, :], i0 + r)
        lb.append(log_beta)
        l1.append(jnp.where(strict2, log_1m, 0.0))
    sums = [jnp.sum(x, axis=1, keepdims=True) for x in l1]
    last = nsub - 1
    out_last, sum_last = tile(q_ref[last * blk:(last + 1) * blk, :], i0 + last, zero_c, strict)
    diag_sum = [s[:blk] for s in sums] + [sum_last]
    prev_valid = i0 >= 1
    out_first, sum_first = tile(q_ref[0:blk, :], jnp.maximum(i0 - 1, 0),
                                diag_sum[0] + jnp.where(prev_valid, 0.0, -1e30), None)
    sum_first = jnp.where(prev_valid, sum_first, 0.0)
    outs = []
    for r in range(nsub - 1):
        carry2 = jnp.concatenate([zero_c, diag_sum[r + 1]], axis=0)
        w, v = weights(lb[r], l1[r], carry2, i0 + r)
        w = jnp.where(strict2, w, 0.0)
        outs.append(_dot(w.astype(BF16), v))
    accs, carries = [], []
    for r in range(nsub):
        diag_out = outs[r][:blk] if r < nsub - 1 else out_last
        prev_out = outs[r - 1][blk:] if r >= 1 else out_first
        prev_sum = sums[r - 1][blk:] if r >= 1 else sum_first
        accs.append(diag_out + prev_out)
        carries.append(diag_sum[r] + prev_sum)

    def step(n, carries, accs):
        new_c, new_a = [], []
        for r in range(nsub):
            q = q_ref[r * blk:(r + 1) * blk, :]
            j = i0 + r - n
            valid = j >= 0
            out, rowsum = tile(q, jnp.maximum(j, 0),
                               carries[r] + jnp.where(valid, 0.0, -1e30), None)
            new_a.append(accs[r] + out)
            new_c.append(carries[r] + jnp.where(valid, rowsum, 0.0))
        return tuple(new_c), tuple(new_a)

    def cond(state):
        n, carries, _ = state
        worst = jnp.full((blk, 1), -1e30, F32)
        for r in range(nsub):
            worst = jnp.maximum(worst, jnp.where(i0 + r - n >= 0, carries[r], -1e30))
        return jnp.max(worst) > EXP_UNDERFLOW

    def body(state):
        n, carries, accs = state
        carries, accs = step(n, carries, accs)
        return n + 1, carries, accs

    _, _, accs = lax.while_loop(cond, body, (jnp.int32(2), tuple(carries), tuple(accs)))
    for r in range(nsub):
        o_ref[r * blk:(r + 1) * blk, :] = accs[r].astype(o_ref.dtype)


def _attention(qk, v, batch, seq, blk, nsub):
    t = qk.shape[0]
    rows = blk * nsub
    nq = seq // rows
    return pl.pallas_call(
        functools.partial(_attn_kernel, blk=blk, nsub=nsub),
        grid=(batch, N_GROUPS, nq),
        in_specs=[pl.BlockSpec((rows, GROUP), lambda b, h, i: (b * nq + i, h)),
                  pl.BlockSpec((seq, GROUP), lambda b, h, i: (b, N_GROUPS + h)),
                  pl.BlockSpec((seq, GROUP), lambda b, h, i: (b, h))],
        out_specs=pl.BlockSpec((rows, GROUP), lambda b, h, i: (b * nq + i, h)),
        out_shape=jax.ShapeDtypeStruct((t, WIDTH), BF16),
        compiler_params=_params(("parallel", "parallel", "arbitrary")),
        name="stick_breaking_attention",
    )(qk, qk, v)


def _sgu_tile(u_ref, v_ref, g_ref, ws_ref, b_ref, y_ref, tm):
    nchunk = tm // GROUP
    row = lax.broadcasted_iota(jnp.int32, (GROUP, GROUP), 0)
    col = lax.broadcasted_iota(jnp.int32, (GROUP, GROUP), 1)
    causal = row >= col
    for g in range(N_GROUPS):
        sl = slice(g * GROUP, (g + 1) * GROUP)
        vn = _rms_rows(v_ref[:, sl].astype(F32), g_ref[:, sl]).astype(BF16)
        ws = jnp.where(causal, ws_ref[g], 0.0).astype(BF16)
        vcat = jnp.concatenate([vn[c * GROUP:(c + 1) * GROUP, :] for c in range(nchunk)], axis=1)
        mixed = _dot(ws, vcat) + b_ref[:, g:g + 1]
        for c in range(nchunk):
            rows = slice(c * GROUP, (c + 1) * GROUP)
            u = u_ref[rows, sl].astype(F32)
            y_ref[rows, sl] = (u * mixed[:, c * GROUP:(c + 1) * GROUP]).astype(y_ref.dtype)


def _merge_kernel(u_ref, v_ref, sg_ref, ws_ref, b_ref, yb_ref, wa_ref, wb_ref, ga_ref, gb_ref, o_ref,
                  ya_ref, wab_ref, wbb_ref, *, tm):
    _cast_weight_once(wa_ref, wab_ref, 0)
    _cast_weight_once(wb_ref, wbb_ref, 0)
    _sgu_tile(u_ref, v_ref, sg_ref, ws_ref, b_ref, ya_ref, tm)
    b = _dot(yb_ref[...], wbb_ref[...])
    a = _dot(ya_ref[...], wab_ref[...])
    o_ref[...] = (ga_ref[...].astype(F32) * a + gb_ref[...].astype(F32) * b).astype(o_ref.dtype)


def _merge(uv, sgu_g, w_s, b_s, yb, wa, wb, gates, tm):
    t = yb.shape[0]
    d = wa.shape[1]
    return pl.pallas_call(
        functools.partial(_merge_kernel, tm=tm),
        grid=(t // tm,),
        in_specs=[pl.BlockSpec((tm, WIDTH), lambda i: (i, 0)),
                  pl.BlockSpec((tm, WIDTH), lambda i: (i, 1)),
                  pl.BlockSpec((1, WIDTH), lambda i: (0, 0)),
                  pl.BlockSpec((N_GROUPS, GROUP, GROUP), lambda i: (0, 0, 0)),
                  pl.BlockSpec((GROUP, N_GROUPS), lambda i: (0, 0)),
                  pl.BlockSpec((tm, WIDTH), lambda i: (i, 0)),
                  pl.BlockSpec((WIDTH, d), lambda i: (0, 0)),
                  pl.BlockSpec((WIDTH, d), lambda i: (0, 0)),
                  pl.BlockSpec((tm, d), lambda i: (i, 0)),
                  pl.BlockSpec((tm, d), lambda i: (i, 1))],
        out_specs=pl.BlockSpec((tm, d), lambda i: (i, 0)),
        out_shape=jax.ShapeDtypeStruct((t, d), BF16),
        scratch_shapes=[pltpu.VMEM((tm, WIDTH), BF16), pltpu.VMEM((WIDTH, d), BF16),
                        pltpu.VMEM((WIDTH, d), BF16)],
        compiler_params=_params(("arbitrary",)),
        name="sgu_gated_merge",
    )(uv, uv, sgu_g.reshape(1, WIDTH), w_s, jnp.transpose(b_s), yb, wa, wb, gates, gates)


def _out_proj_kernel(a_ref, w_ref, r_ref, g_ref, x_ref, h_ref, wb_ref):
    _cast_weight_once(w_ref, wb_ref, 0)
    x = r_ref[...] + _dot(a_ref[...], wb_ref[...])
    x_ref[...] = x
    h_ref[...] = _rms_rows(x, g_ref[...]).astype(h_ref.dtype)


def _out_proj(a, w, r, g, tm):
    t, k = a.shape
    d = w.shape[1]
    return pl.pallas_call(
        _out_proj_kernel,
        grid=(t // tm,),
        in_specs=[pl.BlockSpec((tm, k), lambda i: (i, 0)),
                  pl.BlockSpec((k, d), lambda i: (0, 0)),
                  pl.BlockSpec((tm, d), lambda i: (i, 0)),
                  pl.BlockSpec((1, d), lambda i: (0, 0))],
        out_specs=[pl.BlockSpec((tm, d), lambda i: (i, 0)),
                   pl.BlockSpec((tm, d), lambda i: (i, 0))],
        out_shape=[jax.ShapeDtypeStruct((t, d), F32), jax.ShapeDtypeStruct((t, d), BF16)],
        scratch_shapes=[pltpu.VMEM((k, d), BF16)],
        compiler_params=_params(("arbitrary",)),
        name="out_proj",
    )(a, w, r, g.reshape(1, d))


def _ffn_in_kernel(h_ref, wg_ref, wu_ref, o_ref, wgb_ref, wub_ref):
    _cast_weight_once(wg_ref, wgb_ref, 1)
    _cast_weight_once(wu_ref, wub_ref, 1)
    half = h_ref.shape[0] // 2
    for rows in (slice(0, half), slice(half, 2 * half)):
        h = h_ref[rows, :]
        gate = _dot(h, wgb_ref[...])
        up = _dot(h, wub_ref[...])
        o_ref[rows, :] = (gate * _sigmoid(gate) * up).astype(o_ref.dtype)


def _ffn_in(h, w, hidden, tm, tn):
    t, d = h.shape
    tn = _tile(hidden, tn)
    nb = hidden // tn
    return pl.pallas_call(
        _ffn_in_kernel,
        grid=(nb, t // tm),
        in_specs=[pl.BlockSpec((tm, d), lambda j, i: (i, 0)),
                  pl.BlockSpec((d, tn), lambda j, i: (0, j)),
                  pl.BlockSpec((d, tn), lambda j, i: (0, j + nb))],
        out_specs=pl.BlockSpec((tm, tn), lambda j, i: (i, j)),
        out_shape=jax.ShapeDtypeStruct((t, hidden), BF16),
        scratch_shapes=[pltpu.VMEM((d, tn), BF16), pltpu.VMEM((d, tn), BF16)],
        compiler_params=_params(("parallel", "arbitrary")),
        name="swiglu_in",
    )(h, w, w)


def _matmul_residual_kernel(a_hbm, w_ref, r_ref, o_ref, a_buf, a_sem, *, tm):
    i = pl.program_id(0)
    j = pl.program_id(1)

    def fetch(row_step):
        slot = row_step % 2
        return pltpu.make_async_copy(a_hbm.at[pl.ds(row_step * tm, tm), :], a_buf.at[slot],
                                     a_sem.at[slot])

    @pl.when(jnp.logical_and(i == 0, j == 0))
    def _():
        fetch(0).start()

    @pl.when(j == 0)
    def _():
        @pl.when(i + 1 < pl.num_programs(0))
        def _():
            fetch(i + 1).start()
        fetch(i).wait()

    o_ref[...] = r_ref[...] + _dot(a_buf[i % 2], w_ref[...])


def _matmul_residual(a, w, r, tm, tn, name):
    t, k = a.shape
    d = w.shape[1]
    tn = _tile(d, tn)
    return pl.pallas_call(
        functools.partial(_matmul_residual_kernel, tm=tm),
        grid=(t // tm, d // tn),
        in_specs=[pl.BlockSpec(memory_space=pl.ANY),
                  pl.BlockSpec((k, tn), lambda i, j: (0, j)),
                  pl.BlockSpec((tm, tn), lambda i, j: (i, j))],
        out_specs=pl.BlockSpec((tm, tn), lambda i, j: (i, j)),
        out_shape=jax.ShapeDtypeStruct((t, d), F32),
        scratch_shapes=[pltpu.VMEM((2, tm, k), a.dtype), pltpu.SemaphoreType.DMA((2,))],
        compiler_params=_params(("arbitrary", "arbitrary")),
        name=name,
    )(a, w, r)


def _ple_kernel(x_ref, g_ref, wg_ref, p_ref, wp_ref, o_ref):
    x = x_ref[...]
    r = lax.rsqrt(jnp.mean(x * x, axis=-1, keepdims=True) + EPS)
    gate = jax.nn.sigmoid(r * _dot((x * g_ref[...]).astype(BF16), wg_ref[...]))
    ple = _dot(p_ref[...].astype(BF16), wp_ref[...])
    o_ref[...] = x + gate * ple


def _ple(x, g, wg, p, wp, tm):
    t, d = x.shape
    pd = p.shape[1]
    return pl.pallas_call(
        _ple_kernel,
        grid=(t // tm,),
        in_specs=[pl.BlockSpec((tm, d), lambda i: (i, 0)),
                  pl.BlockSpec((1, d), lambda i: (0, 0)),
                  pl.BlockSpec((d, d), lambda i: (0, 0)),
                  pl.BlockSpec((tm, pd), lambda i: (i, 0)),
                  pl.BlockSpec((pd, d), lambda i: (0, 0))],
        out_specs=pl.BlockSpec((tm, d), lambda i: (i, 0)),
        out_shape=jax.ShapeDtypeStruct((t, d), F32),
        compiler_params=_params(("parallel",)),
        name="gated_ple",
    )(x, g.reshape(1, d), wg, p, wp)


def _layer(x, p, attn_norm_g, w_in, sgu_norm_g, w_s, b_s, q_norm_g, k_norm_g,
           w_up_a, w_up_b, w_o, ffn_norm_g, w_ffn_in, w_ffn_out,
           ple_norm_g, w_ple_gate, w_ple, batch, seq):
    t, d = x.shape
    hidden = w_ffn_out.shape[0]
    tm = _tile(t, 1024)
    tm_full = _tile(t, 512)
    tm_big = _tile(t, 2048)
    blk = _tile(seq, 256)
    nsub = next(n for n in (16, 8, 4, 2, 1) if seq % (n * blk) == 0)

    c_qk = 2 * WIDTH
    c_v = c_qk + 2 * WIDTH
    c_gate = c_v + WIDTH

    uv, h = _proj_sgu(x, attn_norm_g, w_in, c_qk, tm_full)
    scale = GROUP ** -0.5 * LOG2_E
    qk_gain = jnp.concatenate([jnp.tile(q_norm_g * scale, N_GROUPS),
                               jnp.tile(k_norm_g, N_GROUPS)]).reshape(1, 2 * WIDTH)
    qk = _proj(h, w_in, c_qk, 2 * WIDTH, "headnorm", qk_gain, tm, 2048, "proj_qk")
    v_att = _proj(h, w_in, c_v, WIDTH, "none", None, tm_big, 1024, "proj_v")
    gates = _proj(h, w_in, c_gate, 2 * d, "sigmoid", None, tm_big, 1024, "proj_gates")

    y_b = _attention(qk, v_att, batch, seq, blk, nsub)

    merged = _merge(uv, sgu_norm_g.reshape(-1), w_s, b_s, y_b,
                    w_up_a, w_up_b, gates, tm_full)
    x, h = _out_proj(merged, w_o, x, ffn_norm_g, tm_full)

    act = _ffn_in(h, w_ffn_in, hidden, tm_big, 512)
    x = _matmul_residual(act, w_ffn_out.astype(BF16), x, tm, 512, "swiglu_out")

    return _ple(x, ple_norm_g, w_ple_gate.astype(BF16), p, w_ple.astype(BF16), tm_full)


def kernel(x, p, attn_norm_g, w_in, sgu_norm_g, w_s, b_s, q_norm_g, k_norm_g, w_up_a, w_up_b, w_o,
           ffn_norm_g, w_ffn_in, w_ffn_out, ple_norm_g, w_ple_gate, w_ple):
    batch, seq, d = x.shape
    xf = x.reshape(batch * seq, d)
    for i in range(p.shape[0]):
        xf = _layer(xf, p[i].reshape(batch * seq, -1), attn_norm_g[i], w_in[i], sgu_norm_g[i], w_s[i],
                    b_s[i], q_norm_g[i], k_norm_g[i], w_up_a[i], w_up_b[i], w_o[i], ffn_norm_g[i],
                    w_ffn_in[i], w_ffn_out[i], ple_norm_g[i], w_ple_gate[i], w_ple[i], batch, seq)
    return xf.reshape(batch, seq, d)
```

```python
import functools

import jax
import jax.numpy as jnp
from jax import lax
from jax.experimental import pallas as pl
from jax.experimental.pallas import tpu as pltpu

F32 = jnp.float32
BF16 = jnp.bfloat16

EPS = 1e-6
GROUP = 128
N_GROUPS = 8
WIDTH = GROUP * N_GROUPS

V7X_VMEM_BYTES = 64 * 1024 * 1024
VMEM_LIMIT = V7X_VMEM_BYTES - 8 * 1024 * 1024

LOG2_E = 1.4426950408889634
GELU_C1 = 0.7978845608028654
GELU_C2 = GELU_C1 * 0.044715
EXP_UNDERFLOW = -150.0


def _tile(n, want):
    if n <= want:
        return n
    t = (want // 128) * 128
    while t >= 128:
        if n % t == 0:
            return t
        t -= 128
    return n


def _params(sem):
    return pltpu.CompilerParams(dimension_semantics=sem, vmem_limit_bytes=VMEM_LIMIT)


def _dot(a, b):
    return jnp.dot(a, b, preferred_element_type=F32)


def _sigmoid(x):
    return 0.5 * jnp.tanh(0.5 * x) + 0.5


def _gelu(x):
    return x * (0.5 * jnp.tanh(x * (GELU_C1 + GELU_C2 * (x * x))) + 0.5)


def _rms_rows(x, g):
    y = x * lax.rsqrt(jnp.mean(x * x, axis=-1, keepdims=True) + EPS)
    return y * g


def _zero_after(x):
    bits = pltpu.bitcast(x, jnp.uint32)
    zero = lax.shift_right_logical(lax.shift_right_logical(bits, jnp.uint32(16)), jnp.uint32(16))
    return jnp.max(zero.astype(F32), axis=0, keepdims=True)


def _cast_weight_once(w_ref, wb_ref, axis):
    @pl.when(pl.program_id(axis) == 0)
    def _():
        wb_ref[...] = w_ref[...].astype(wb_ref.dtype)


def _proj_sgu_kernel(x_ref, g_ref, w_ref, o_ref, h_ref, wb_ref):
    _cast_weight_once(w_ref, wb_ref, 0)
    h = _rms_rows(x_ref[...], g_ref[...]).astype(h_ref.dtype)
    h_ref[...] = h
    o_ref[...] = _gelu(_dot(h, wb_ref[...])).astype(o_ref.dtype)


def _proj_sgu(x, g, w, ncols, tm):
    t, d = x.shape
    return pl.pallas_call(
        _proj_sgu_kernel,
        grid=(t // tm,),
        in_specs=[pl.BlockSpec((tm, d), lambda i: (i, 0)),
                  pl.BlockSpec((1, d), lambda i: (0, 0)),
                  pl.BlockSpec((d, ncols), lambda i: (0, 0))],
        out_specs=[pl.BlockSpec((tm, ncols), lambda i: (i, 0)),
                   pl.BlockSpec((tm, d), lambda i: (i, 0))],
        out_shape=[jax.ShapeDtypeStruct((t, ncols), BF16), jax.ShapeDtypeStruct((t, d), BF16)],
        scratch_shapes=[pltpu.VMEM((d, ncols), BF16)],
        compiler_params=_params(("arbitrary",)),
        name="proj_sgu",
    )(x, g.reshape(1, d), w)


def _proj_kernel(h_ref, w_ref, gain_ref, o_ref, wb_ref, *, kind):
    _cast_weight_once(w_ref, wb_ref, 1)
    if kind == "sigmoid":
        half = h_ref.shape[0] // 2
        for rows in (slice(0, half), slice(half, 2 * half)):
            o_ref[rows, :] = _sigmoid(_dot(h_ref[rows, :], wb_ref[...])).astype(o_ref.dtype)
        return
    acc = _dot(h_ref[...], wb_ref[...])
    if kind == "none":
        o_ref[...] = acc.astype(o_ref.dtype)
    elif kind == "headnorm":
        gain = gain_ref[...]
        for c in range(acc.shape[1] // GROUP):
            sl = slice(c * GROUP, (c + 1) * GROUP)
            o_ref[:, sl] = _rms_rows(acc[:, sl], gain[:, sl]).astype(o_ref.dtype)
    else:
        raise ValueError(kind)


def _proj(h, w, col0, ncols, kind, gain, tm, tn, name):
    t, d = h.shape
    tn = _tile(ncols, tn)
    assert col0 % tn == 0
    off = col0 // tn
    if gain is None:
        gain = jnp.ones((1, ncols), F32)
    return pl.pallas_call(
        functools.partial(_proj_kernel, kind=kind),
        grid=(ncols // tn, t // tm),
        in_specs=[pl.BlockSpec((tm, d), lambda j, i: (i, 0)),
                  pl.BlockSpec((d, tn), lambda j, i: (0, j + off)),
                  pl.BlockSpec((1, tn), lambda j, i: (0, j))],
        out_specs=pl.BlockSpec((tm, tn), lambda j, i: (i, j)),
        out_shape=jax.ShapeDtypeStruct((t, ncols), BF16),
        scratch_shapes=[pltpu.VMEM((d, tn), BF16)],
        compiler_params=_params(("parallel", "arbitrary")),
        name=name,
    )(h, w, gain)


def _attn_kernel(q_ref, k_ref, v_ref, o_ref, *, blk, nsub):
    i0 = pl.program_id(2) * nsub
    row = lax.broadcasted_iota(jnp.int32, (blk, blk), 0)
    col = lax.broadcasted_iota(jnp.int32, (blk, blk), 1)
    strict = col < row
    later = strict.astype(BF16)
    row2 = lax.broadcasted_iota(jnp.int32, (2 * blk, blk), 0)
    col2 = lax.broadcasted_iota(jnp.int32, (2 * blk, blk), 1)
    strict2 = jnp.logical_or(col2 < row2, row2 >= blk)

    def scores(q, j):
        start = pl.multiple_of(j * blk, blk)
        k = k_ref[pl.ds(start, blk), :]
        z = lax.dot_general(q, k, (((1,), (1,)), ((), ())), preferred_element_type=F32)
        log_beta = jnp.minimum(z, 0.0) - LOG2_E * jnp.log(1.0 + jnp.exp2(-jnp.abs(z)))
        return log_beta, log_beta - z

    def weights(log_beta, log_1m, carry, j):
        start = pl.multiple_of(j * blk, blk)
        v = v_ref[pl.ds(start, blk), :]
        tail = _dot(log_1m.astype(BF16), later) + carry
        return jnp.exp2(log_beta + tail), v

    def tile(q, j, carry, mask):
        log_beta, log_1m = scores(q, j)
        if mask is not None:
            log_1m = jnp.where(mask, log_1m, 0.0)
        w, v = weights(log_beta, log_1m, carry, j)
        if mask is not None:
            w = jnp.where(mask, w, 0.0)
        return _dot(w.astype(BF16), v), jnp.sum(log_1m, axis=1, keepdims=True)

    zero_c = jnp.zeros((blk, 1), F32)

    lb, l1 = [], []
    for r in range(nsub - 1):
        log_beta, log_1m = scores(q_ref[r * blk:(r + 2) * blk, :], i0 + r)
        lb.append(log_beta)
        l1.append(jnp.where(strict2, log_1m, 0.0))
    sums = [jnp.sum(x, axis=1, keepdims=True) for x in l1]
    last = nsub - 1
    out_last, sum_last = tile(q_ref[last * blk:(last + 1) * blk, :], i0 + last, zero_c, strict)
    diag_sum = [s[:blk] for s in sums] + [sum_last]
    prev_valid = i0 >= 1
    out_first, sum_first = tile(q_ref[0:blk, :], jnp.maximum(i0 - 1, 0),
                                diag_sum[0] + jnp.where(prev_valid, 0.0, -1e30), None)
    sum_first = jnp.where(prev_valid, sum_first, 0.0)
    outs = []
    for r in range(nsub - 1):
        carry2 = jnp.concatenate([zero_c, diag_sum[r + 1]], axis=0)
        w, v = weights(lb[r], l1[r], carry2, i0 + r)
        w = jnp.where(strict2, w, 0.0)
        outs.append(_dot(w.astype(BF16), v))
    accs, carries = [], []
    for r in range(nsub):
        diag_out = outs[r][:blk] if r < nsub - 1 else out_last
        prev_out = outs[r - 1][blk:] if r >= 1 else out_first
        prev_sum = sums[r - 1][blk:] if r >= 1 else sum_first
        accs.append(diag_out + prev_out)
        carries.append(diag_sum[r] + prev_sum)

    def step(n, carries, accs):
        new_c, new_a = [], []
        for r in range(nsub):
            q = q_ref[r * blk:(r + 1) * blk, :]
            j = i0 + r - n
            valid = j >= 0
            out, rowsum = tile(q, jnp.maximum(j, 0),
                               carries[r] + jnp.where(valid, 0.0, -1e30), None)
            new_a.append(accs[r] + out)
            new_c.append(carries[r] + jnp.where(valid, rowsum, 0.0))
        return tuple(new_c), tuple(new_a)

    def cond(state):
        n, carries, _ = state
        worst = jnp.full((blk, 1), -1e30, F32)
        for r in range(nsub):
            worst = jnp.maximum(worst, jnp.where(i0 + r - n >= 0, carries[r], -1e30))
        return jnp.max(worst) > EXP_UNDERFLOW

    def body(state):
        n, carries, accs = state
        carries, accs = step(n, carries, accs)
        return n + 1, carries, accs

    _, _, accs = lax.while_loop(cond, body, (jnp.int32(2), tuple(carries), tuple(accs)))
    for r in range(nsub):
        o_ref[r * blk:(r + 1) * blk, :] = accs[r].astype(o_ref.dtype)


def _attention(qk, v, batch, seq, blk, nsub):
    t = qk.shape[0]
    rows = blk * nsub
    nq = seq // rows
    return pl.pallas_call(
        functools.partial(_attn_kernel, blk=blk, nsub=nsub),
        grid=(batch, N_GROUPS, nq),
        in_specs=[pl.BlockSpec((rows, GROUP), lambda b, h, i: (b * nq + i, h)),
                  pl.BlockSpec((seq, GROUP), lambda b, h, i: (b, N_GROUPS + h)),
                  pl.BlockSpec((seq, GROUP), lambda b, h, i: (b, h))],
        out_specs=pl.BlockSpec((rows, GROUP), lambda b, h, i: (b * nq + i, h)),
        out_shape=jax.ShapeDtypeStruct((t, WIDTH), BF16),
        compiler_params=_params(("parallel", "parallel", "arbitrary")),
        name="stick_breaking_attention",
    )(qk, qk, v)


def _sgu_tile(u_ref, v_ref, g_ref, ws_ref, b_ref, y_ref, tm, zero):
    nchunk = tm // GROUP
    row = lax.broadcasted_iota(jnp.int32, (GROUP, GROUP), 0)
    col = lax.broadcasted_iota(jnp.int32, (GROUP, GROUP), 1)
    causal = row >= col
    for g in range(N_GROUPS):
        sl = slice(g * GROUP, (g + 1) * GROUP)
        vn = _rms_rows(v_ref[:, sl].astype(F32) + zero, g_ref[:, sl]).astype(BF16)
        ws = jnp.where(causal, ws_ref[g], 0.0).astype(BF16)
        vcat = jnp.concatenate([vn[c * GROUP:(c + 1) * GROUP, :] for c in range(nchunk)], axis=1)
        mixed = _dot(ws, vcat) + b_ref[:, g:g + 1]
        for c in range(nchunk):
            rows = slice(c * GROUP, (c + 1) * GROUP)
            u = u_ref[rows, sl].astype(F32)
            y_ref[rows, sl] = (u * mixed[:, c * GROUP:(c + 1) * GROUP]).astype(y_ref.dtype)


def _merge_kernel(u_ref, v_ref, sg_ref, ws_ref, b_ref, yb_ref, wa_ref, wb_ref, ga_ref, gb_ref, o_ref,
                  ya_ref, wab_ref, wbb_ref, *, tm):
    _cast_weight_once(wa_ref, wab_ref, 0)
    _cast_weight_once(wb_ref, wbb_ref, 0)
    b = _dot(yb_ref[...], wbb_ref[...])
    _sgu_tile(u_ref, v_ref, sg_ref, ws_ref, b_ref, ya_ref, tm, _zero_after(b[0:8, 0:GROUP]))
    a = _dot(ya_ref[...], wab_ref[...])
    o_ref[...] = (ga_ref[...].astype(F32) * a + gb_ref[...].astype(F32) * b).astype(o_ref.dtype)


def _merge(uv, sgu_g, w_s, b_s, yb, wa, wb, gates, tm):
    t = yb.shape[0]
    d = wa.shape[1]
    return pl.pallas_call(
        functools.partial(_merge_kernel, tm=tm),
        grid=(t // tm,),
        in_specs=[pl.BlockSpec((tm, WIDTH), lambda i: (i, 0)),
                  pl.BlockSpec((tm, WIDTH), lambda i: (i, 1)),
                  pl.BlockSpec((1, WIDTH), lambda i: (0, 0)),
                  pl.BlockSpec((N_GROUPS, GROUP, GROUP), lambda i: (0, 0, 0)),
                  pl.BlockSpec((GROUP, N_GROUPS), lambda i: (0, 0)),
                  pl.BlockSpec((tm, WIDTH), lambda i: (i, 0)),
                  pl.BlockSpec((WIDTH, d), lambda i: (0, 0)),
                  pl.BlockSpec((WIDTH, d), lambda i: (0, 0)),
                  pl.BlockSpec((tm, d), lambda i: (i, 0)),
                  pl.BlockSpec((tm, d), lambda i: (i, 1))],
        out_specs=pl.BlockSpec((tm, d), lambda i: (i, 0)),
        out_shape=jax.ShapeDtypeStruct((t, d), BF16),
        scratch_shapes=[pltpu.VMEM((tm, WIDTH), BF16), pltpu.VMEM((WIDTH, d), BF16),
                        pltpu.VMEM((WIDTH, d), BF16)],
        compiler_params=_params(("arbitrary",)),
        name="sgu_gated_merge",
    )(uv, uv, sgu_g.reshape(1, WIDTH), w_s, jnp.transpose(b_s), yb, wa, wb, gates, gates)


def _out_proj_kernel(a_ref, w_ref, r_ref, g_ref, x_ref, h_ref, wb_ref):
    _cast_weight_once(w_ref, wb_ref, 0)
    x = r_ref[...] + _dot(a_ref[...], wb_ref[...])
    x_ref[...] = x
    h_ref[...] = _rms_rows(x, g_ref[...]).astype(h_ref.dtype)


def _out_proj(a, w, r, g, tm):
    t, k = a.shape
    d = w.shape[1]
    return pl.pallas_call(
        _out_proj_kernel,
        grid=(t // tm,),
        in_specs=[pl.BlockSpec((tm, k), lambda i: (i, 0)),
                  pl.BlockSpec((k, d), lambda i: (0, 0)),
                  pl.BlockSpec((tm, d), lambda i: (i, 0)),
                  pl.BlockSpec((1, d), lambda i: (0, 0))],
        out_specs=[pl.BlockSpec((tm, d), lambda i: (i, 0)),
                   pl.BlockSpec((tm, d), lambda i: (i, 0))],
        out_shape=[jax.ShapeDtypeStruct((t, d), F32), jax.ShapeDtypeStruct((t, d), BF16)],
        scratch_shapes=[pltpu.VMEM((k, d), BF16)],
        compiler_params=_params(("arbitrary",)),
        name="out_proj",
    )(a, w, r, g.reshape(1, d))


def _ffn_in_kernel(h_ref, wg_ref, wu_ref, o_ref, wgb_ref, wub_ref):
    _cast_weight_once(wg_ref, wgb_ref, 1)
    _cast_weight_once(wu_ref, wub_ref, 1)
    half = h_ref.shape[0] // 2
    for rows in (slice(0, half), slice(half, 2 * half)):
        h = h_ref[rows, :]
        gate = _dot(h, wgb_ref[...])
        up = _dot(h, wub_ref[...])
        o_ref[rows, :] = (gate * _sigmoid(gate) * up).astype(o_ref.dtype)


def _ffn_in(h, w, hidden, tm, tn):
    t, d = h.shape
    tn = _tile(hidden, tn)
    nb = hidden // tn
    return pl.pallas_call(
        _ffn_in_kernel,
        grid=(nb, t // tm),
        in_specs=[pl.BlockSpec((tm, d), lambda j, i: (i, 0)),
                  pl.BlockSpec((d, tn), lambda j, i: (0, j)),
                  pl.BlockSpec((d, tn), lambda j, i: (0, j + nb))],
        out_specs=pl.BlockSpec((tm, tn), lambda j, i: (i, j)),
        out_shape=jax.ShapeDtypeStruct((t, hidden), BF16),
        scratch_shapes=[pltpu.VMEM((d, tn), BF16), pltpu.VMEM((d, tn), BF16)],
        compiler_params=_params(("parallel", "arbitrary")),
        name="swiglu_in",
    )(h, w, w)


def _matmul_residual_kernel(a_hbm, w_ref, r_ref, o_ref, a_buf, a_sem, *, tm):
    i = pl.program_id(0)
    j = pl.program_id(1)

    def fetch(row_step):
        slot = row_step % 2
        return pltpu.make_async_copy(a_hbm.at[pl.ds(row_step * tm, tm), :], a_buf.at[slot],
                                     a_sem.at[slot])

    @pl.when(jnp.logical_and(i == 0, j == 0))
    def _():
        fetch(0).start()

    @pl.when(j == 0)
    def _():
        @pl.when(i + 1 < pl.num_programs(0))
        def _():
            fetch(i + 1).start()
        fetch(i).wait()

    o_ref[...] = r_ref[...] + _dot(a_buf[i % 2], w_ref[...])


def _matmul_residual(a, w, r, tm, tn, name):
    t, k = a.shape
    d = w.shape[1]
    tn = _tile(d, tn)
    return pl.pallas_call(
        functools.partial(_matmul_residual_kernel, tm=tm),
        grid=(t // tm, d // tn),
        in_specs=[pl.BlockSpec(memory_space=pl.ANY),
                  pl.BlockSpec((k, tn), lambda i, j: (0, j)),
                  pl.BlockSpec((tm, tn), lambda i, j: (i, j))],
        out_specs=pl.BlockSpec((tm, tn), lambda i, j: (i, j)),
        out_shape=jax.ShapeDtypeStruct((t, d), F32),
        scratch_shapes=[pltpu.VMEM((2, tm, k), a.dtype), pltpu.SemaphoreType.DMA((2,))],
        compiler_params=_params(("arbitrary", "arbitrary")),
        name=name,
    )(a, w, r)


def _ple_kernel(x_ref, g_ref, wg_ref, p_ref, wp_ref, o_ref):
    x = x_ref[...]
    r = lax.rsqrt(jnp.mean(x * x, axis=-1, keepdims=True) + EPS)
    gate = jax.nn.sigmoid(r * _dot((x * g_ref[...]).astype(BF16), wg_ref[...]))
    ple = _dot(p_ref[...].astype(BF16), wp_ref[...])
    o_ref[...] = x + gate * ple


def _ple(x, g, wg, p, wp, tm):
    t, d = x.shape
    pd = p.shape[1]
    return pl.pallas_call(
        _ple_kernel,
        grid=(t // tm,),
        in_specs=[pl.BlockSpec((tm, d), lambda i: (i, 0)),
                  pl.BlockSpec((1, d), lambda i: (0, 0)),
                  pl.BlockSpec((d, d), lambda i: (0, 0)),
                  pl.BlockSpec((tm, pd), lambda i: (i, 0)),
                  pl.BlockSpec((pd, d), lambda i: (0, 0))],
        out_specs=pl.BlockSpec((tm, d), lambda i: (i, 0)),
        out_shape=jax.ShapeDtypeStruct((t, d), F32),
        compiler_params=_params(("parallel",)),
        name="gated_ple",
    )(x, g.reshape(1, d), wg, p, wp)


def _layer(x, p, attn_norm_g, w_in, sgu_norm_g, w_s, b_s, q_norm_g, k_norm_g,
           w_up_a, w_up_b, w_o, ffn_norm_g, w_ffn_in, w_ffn_out,
           ple_norm_g, w_ple_gate, w_ple, batch, seq):
    t, d = x.shape
    hidden = w_ffn_out.shape[0]
    tm = _tile(t, 1024)
    tm_full = _tile(t, 512)
    tm_big = _tile(t, 2048)
    blk = _tile(seq, 256)
    nsub = next(n for n in (16, 8, 4, 2, 1) if seq % (n * blk) == 0)

    c_qk = 2 * WIDTH
    c_v = c_qk + 2 * WIDTH
    c_gate = c_v + WIDTH

    uv, h = _proj_sgu(x, attn_norm_g, w_in, c_qk, tm_full)
    scale = GROUP ** -0.5 * LOG2_E
    qk_gain = jnp.concatenate([jnp.tile(q_norm_g * scale, N_GROUPS),
                               jnp.tile(k_norm_g, N_GROUPS)]).reshape(1, 2 * WIDTH)
    qk = _proj(h, w_in, c_qk, 2 * WIDTH, "headnorm", qk_gain, tm, 2048, "proj_qk")
    v_att = _proj(h, w_in, c_v, WIDTH, "none", None, tm_big, 1024, "proj_v")
    gates = _proj(h, w_in, c_gate, 2 * d, "sigmoid", None, tm_big, 1024, "proj_gates")

    y_b = _attention(qk, v_att, batch, seq, blk, nsub)

    merged = _merge(uv, sgu_norm_g.reshape(-1), w_s, b_s, y_b,
                    w_up_a, w_up_b, gates, tm_full)
    x, h = _out_proj(merged, w_o, x, ffn_norm_g, tm_full)

    act = _ffn_in(h, w_ffn_in, hidden, tm_big, 512)
    x = _matmul_residual(act, w_ffn_out.astype(BF16), x, tm, 512, "swiglu_out")

    return _ple(x, ple_norm_g, w_ple_gate.astype(BF16), p, w_ple.astype(BF16), tm_full)


def kernel(x, p, attn_norm_g, w_in, sgu_norm_g, w_s, b_s, q_norm_g, k_norm_g, w_up_a, w_up_b, w_o,
           ffn_norm_g, w_ffn_in, w_ffn_out, ple_norm_g, w_ple_gate, w_ple):
    batch, seq, d = x.shape
    xf = x.reshape(batch * seq, d)
    for i in range(p.shape[0]):
        xf = _layer(xf, p[i].reshape(batch * seq, -1), attn_norm_g[i], w_in[i], sgu_norm_g[i], w_s[i],
                    b_s[i], q_norm_g[i], k_norm_g[i], w_up_a[i], w_up_b[i], w_o[i], ffn_norm_g[i],
                    w_ffn_in[i], w_ffn_out[i], ple_norm_g[i], w_ple_gate[i], w_ple[i], batch, seq)
    return xf.reshape(batch, seq, d)
```

```python
import functools

import jax
import jax.numpy as jnp
from jax import lax
from jax.experimental import pallas as pl
from jax.experimental.pallas import tpu as pltpu

F32 = jnp.float32
BF16 = jnp.bfloat16

EPS = 1e-6
GROUP = 128
N_GROUPS = 8
WIDTH = GROUP * N_GROUPS

V7X_VMEM_BYTES = 64 * 1024 * 1024
VMEM_LIMIT = V7X_VMEM_BYTES - 8 * 1024 * 1024

LOG2_E = 1.4426950408889634
GELU_C1 = 0.7978845608028654
GELU_C2 = GELU_C1 * 0.044715
EXP_UNDERFLOW = -150.0


def _tile(n, want):
    if n <= want:
        return n
    t = (want // 128) * 128
    while t >= 128:
        if n % t == 0:
            return t
        t -= 128
    return n


def _params(sem):
    return pltpu.CompilerParams(dimension_semantics=sem, vmem_limit_bytes=VMEM_LIMIT)


def _dot(a, b):
    return jnp.dot(a, b, preferred_element_type=F32)


def _sigmoid(x):
    return 0.5 * jnp.tanh(0.5 * x) + 0.5


def _gelu(x):
    return x * (0.5 * jnp.tanh(x * (GELU_C1 + GELU_C2 * (x * x))) + 0.5)


def _rms_rows(x, g):
    y = x * lax.rsqrt(jnp.mean(x * x, axis=-1, keepdims=True) + EPS)
    return y * g


def _zero_after(x):
    bits = pltpu.bitcast(x, jnp.uint32)
    zero = lax.shift_right_logical(lax.shift_right_logical(bits, jnp.uint32(16)), jnp.uint32(16))
    return jnp.max(zero.astype(F32), axis=0, keepdims=True)


def _cast_weight_once(w_ref, wb_ref, axis):
    @pl.when(pl.program_id(axis) == 0)
    def _():
        wb_ref[...] = w_ref[...].astype(wb_ref.dtype)


def _proj_sgu_kernel(x_ref, g_ref, w_ref, o_ref, h_ref, wb_ref):
    _cast_weight_once(w_ref, wb_ref, 0)
    h = _rms_rows(x_ref[...], g_ref[...]).astype(h_ref.dtype)
    h_ref[...] = h
    o_ref[...] = _gelu(_dot(h, wb_ref[...])).astype(o_ref.dtype)


def _proj_sgu(x, g, w, ncols, tm):
    t, d = x.shape
    return pl.pallas_call(
        _proj_sgu_kernel,
        grid=(t // tm,),
        in_specs=[pl.BlockSpec((tm, d), lambda i: (i, 0)),
                  pl.BlockSpec((1, d), lambda i: (0, 0)),
                  pl.BlockSpec((d, ncols), lambda i: (0, 0))],
        out_specs=[pl.BlockSpec((tm, ncols), lambda i: (i, 0)),
                   pl.BlockSpec((tm, d), lambda i: (i, 0))],
        out_shape=[jax.ShapeDtypeStruct((t, ncols), BF16), jax.ShapeDtypeStruct((t, d), BF16)],
        scratch_shapes=[pltpu.VMEM((d, ncols), BF16)],
        compiler_params=_params(("arbitrary",)),
        name="proj_sgu",
    )(x, g.reshape(1, d), w)


def _proj_kernel(h_ref, w_ref, gain_ref, o_ref, wb_ref, *, kind):
    _cast_weight_once(w_ref, wb_ref, 1)
    if kind == "sigmoid":
        half = h_ref.shape[0] // 2
        for rows in (slice(0, half), slice(half, 2 * half)):
            o_ref[rows, :] = _sigmoid(_dot(h_ref[rows, :], wb_ref[...])).astype(o_ref.dtype)
        return
    acc = _dot(h_ref[...], wb_ref[...])
    if kind == "none":
        o_ref[...] = acc.astype(o_ref.dtype)
    elif kind == "headnorm":
        gain = gain_ref[...]
        for c in range(acc.shape[1] // GROUP):
            sl = slice(c * GROUP, (c + 1) * GROUP)
            o_ref[:, sl] = _rms_rows(acc[:, sl], gain[:, sl]).astype(o_ref.dtype)
    else:
        raise ValueError(kind)


def _proj(h, w, col0, ncols, kind, gain, tm, tn, name):
    t, d = h.shape
    tn = _tile(ncols, tn)
    assert col0 % tn == 0
    off = col0 // tn
    if gain is None:
        gain = jnp.ones((1, ncols), F32)
    return pl.pallas_call(
        functools.partial(_proj_kernel, kind=kind),
        grid=(ncols // tn, t // tm),
        in_specs=[pl.BlockSpec((tm, d), lambda j, i: (i, 0)),
                  pl.BlockSpec((d, tn), lambda j, i: (0, j + off)),
                  pl.BlockSpec((1, tn), lambda j, i: (0, j))],
        out_specs=pl.BlockSpec((tm, tn), lambda j, i: (i, j)),
        out_shape=jax.ShapeDtypeStruct((t, ncols), BF16),
        scratch_shapes=[pltpu.VMEM((d, tn), BF16)],
        compiler_params=_params(("parallel", "arbitrary")),
        name=name,
    )(h, w, gain)


def _attn_kernel(q_ref, k_ref, v_ref, o_ref, *, blk, nsub):
    i0 = pl.program_id(2) * nsub
    row = lax.broadcasted_iota(jnp.int32, (blk, blk), 0)
    col = lax.broadcasted_iota(jnp.int32, (blk, blk), 1)
    strict = col < row
    later = strict.astype(BF16)
    row2 = lax.broadcasted_iota(jnp.int32, (2 * blk, blk), 0)
    col2 = lax.broadcasted_iota(jnp.int32, (2 * blk, blk), 1)
    strict2 = jnp.logical_or(col2 < row2, row2 >= blk)

    def scores(q, j):
        start = pl.multiple_of(j * blk, blk)
        k = k_ref[pl.ds(start, blk), :]
        z = lax.dot_general(q, k, (((1,), (1,)), ((), ())), preferred_element_type=F32)
        log_beta = jnp.minimum(z, 0.0) - LOG2_E * jnp.log(1.0 + jnp.exp2(-jnp.abs(z)))
        return log_beta, log_beta - z

    def weights(log_beta, log_1m, carry, j):
        start = pl.multiple_of(j * blk, blk)
        v = v_ref[pl.ds(start, blk), :]
        tail = _dot(log_1m.astype(BF16), later) + carry
        return jnp.exp2(log_beta + tail), v

    def tile(q, j, carry, mask):
        log_beta, log_1m = scores(q, j)
        if mask is not None:
            log_1m = jnp.where(mask, log_1m, 0.0)
        w, v = weights(log_beta, log_1m, carry, j)
        if mask is not None:
            w = jnp.where(mask, w, 0.0)
        return _dot(w.astype(BF16), v), jnp.sum(log_1m, axis=1, keepdims=True)

    zero_c = jnp.zeros((blk, 1), F32)

    lb, l1 = [], []
    for r in range(nsub - 1):
        log_beta, log_1m = scores(q_ref[r * blk:(r + 2) * blk, :], i0 + r)
        lb.append(log_beta)
        l1.append(jnp.where(strict2, log_1m, 0.0))
    sums = [jnp.sum(x, axis=1, keepdims=True) for x in l1]
    last = nsub - 1
    out_last, sum_last = tile(q_ref[last * blk:(last + 1) * blk, :], i0 + last, zero_c, strict)
    diag_sum = [s[:blk] for s in sums] + [sum_last]
    prev_valid = i0 >= 1
    out_first, sum_first = tile(q_ref[0:blk, :], jnp.maximum(i0 - 1, 0),
                                diag_sum[0] + jnp.where(prev_valid, 0.0, -1e30), None)
    sum_first = jnp.where(prev_valid, sum_first, 0.0)
    outs = []
    for r in range(nsub - 1):
        carry2 = jnp.concatenate([zero_c, diag_sum[r + 1]], axis=0)
        w, v = weights(lb[r], l1[r], carry2, i0 + r)
        w = jnp.where(strict2, w, 0.0)
        outs.append(_dot(w.astype(BF16), v))
    accs, carries = [], []
    for r in range(nsub):
        diag_out = outs[r][:blk] if r < nsub - 1 else out_last
        prev_out = outs[r - 1][blk:] if r >= 1 else out_first
        prev_sum = sums[r - 1][blk:] if r >= 1 else sum_first
        accs.append(diag_out + prev_out)
        carries.append(diag_sum[r] + prev_sum)

    def step(n, carries, accs):
        new_c, new_a = [], []
        for r in range(nsub):
            q = q_ref[r * blk:(r + 1) * blk, :]
            j = i0 + r - n
            valid = j >= 0
            out, rowsum = tile(q, jnp.maximum(j, 0),
                               carries[r] + jnp.where(valid, 0.0, -1e30), None)
            new_a.append(accs[r] + out)
            new_c.append(carries[r] + jnp.where(valid, rowsum, 0.0))
        return tuple(new_c), tuple(new_a)

    def cond(state):
        n, carries, _ = state
        worst = jnp.full((blk, 1), -1e30, F32)
        for r in range(nsub):
            worst = jnp.maximum(worst, jnp.where(i0 + r - n >= 0, carries[r], -1e30))
        return jnp.max(worst) > EXP_UNDERFLOW

    def body(state):
        n, carries, accs = state
        carries, accs = step(n, carries, accs)
        return n + 1, carries, accs

    _, _, accs = lax.while_loop(cond, body, (jnp.int32(2), tuple(carries), tuple(accs)))
    for r in range(nsub):
        o_ref[r * blk:(r + 1) * blk, :] = accs[r].astype(o_ref.dtype)


def _attention(qk, v, batch, seq, blk, nsub):
    t = qk.shape[0]
    rows = blk * nsub
    nq = seq // rows
    return pl.pallas_call(
        functools.partial(_attn_kernel, blk=blk, nsub=nsub),
        grid=(batch, N_GROUPS, nq),
        in_specs=[pl.BlockSpec((rows, GROUP), lambda b, h, i: (b * nq + i, h)),
                  pl.BlockSpec((seq, GROUP), lambda b, h, i: (b, N_GROUPS + h)),
                  pl.BlockSpec((seq, GROUP), lambda b, h, i: (b, h))],
        out_specs=pl.BlockSpec((rows, GROUP), lambda b, h, i: (b * nq + i, h)),
        out_shape=jax.ShapeDtypeStruct((t, WIDTH), BF16),
        compiler_params=_params(("parallel", "parallel", "arbitrary")),
        name="stick_breaking_attention",
    )(qk, qk, v)


def _sgu_tile(u_ref, v_ref, g_ref, ws_ref, b_ref, y_ref, tm, zero):
    nchunk = tm // GROUP
    row = lax.broadcasted_iota(jnp.int32, (GROUP, GROUP), 0)
    col = lax.broadcasted_iota(jnp.int32, (GROUP, GROUP), 1)
    causal = row >= col
    for g in range(N_GROUPS):
        sl = slice(g * GROUP, (g + 1) * GROUP)
        vn = _rms_rows(v_ref[:, sl].astype(F32) + zero, g_ref[:, sl]).astype(BF16)
        ws = jnp.where(causal, ws_ref[g], 0.0).astype(BF16)
        vcat = jnp.concatenate([vn[c * GROUP:(c + 1) * GROUP, :] for c in range(nchunk)], axis=1)
        mixed = _dot(ws, vcat) + b_ref[:, g:g + 1]
        for c in range(nchunk):
            rows = slice(c * GROUP, (c + 1) * GROUP)
            u = u_ref[rows, sl].astype(F32)
            y_ref[rows, sl] = (u * mixed[:, c * GROUP:(c + 1) * GROUP]).astype(y_ref.dtype)


def _merge_kernel(u_ref, v_ref, sg_ref, ws_ref, b_ref, yb_ref, wa_ref, wb_ref, ga_ref, gb_ref, o_ref,
                  ya_ref, wab_ref, wbb_ref, *, tm):
    _cast_weight_once(wa_ref, wab_ref, 0)
    _cast_weight_once(wb_ref, wbb_ref, 0)
    b = _dot(yb_ref[...], wbb_ref[...])
    _sgu_tile(u_ref, v_ref, sg_ref, ws_ref, b_ref, ya_ref, tm, _zero_after(b[0:8, 0:GROUP]))
    a = _dot(ya_ref[...], wab_ref[...])
    o_ref[...] = (ga_ref[...].astype(F32) * a + gb_ref[...].astype(F32) * b).astype(o_ref.dtype)


def _merge(uv, sgu_g, w_s, b_s, yb, wa, wb, gates, tm):
    t = yb.shape[0]
    d = wa.shape[1]
    return pl.pallas_call(
        functools.partial(_merge_kernel, tm=tm),
        grid=(t // tm,),
        in_specs=[pl.BlockSpec((tm, WIDTH), lambda i: (i, 0)),
                  pl.BlockSpec((tm, WIDTH), lambda i: (i, 1)),
                  pl.BlockSpec((1, WIDTH), lambda i: (0, 0)),
                  pl.BlockSpec((N_GROUPS, GROUP, GROUP), lambda i: (0, 0, 0)),
                  pl.BlockSpec((GROUP, N_GROUPS), lambda i: (0, 0)),
                  pl.BlockSpec((tm, WIDTH), lambda i: (i, 0)),
                  pl.BlockSpec((WIDTH, d), lambda i: (0, 0)),
                  pl.BlockSpec((WIDTH, d), lambda i: (0, 0)),
                  pl.BlockSpec((tm, d), lambda i: (i, 0)),
                  pl.BlockSpec((tm, d), lambda i: (i, 1))],
        out_specs=pl.BlockSpec((tm, d), lambda i: (i, 0)),
        out_shape=jax.ShapeDtypeStruct((t, d), BF16),
        scratch_shapes=[pltpu.VMEM((tm, WIDTH), BF16), pltpu.VMEM((WIDTH, d), BF16),
                        pltpu.VMEM((WIDTH, d), BF16)],
        compiler_params=_params(("arbitrary",)),
        name="sgu_gated_merge",
    )(uv, uv, sgu_g.reshape(1, WIDTH), w_s, jnp.transpose(b_s), yb, wa, wb, gates, gates)


def _out_proj_kernel(a_ref, w_ref, r_ref, g_ref, x_ref, h_ref, wb_ref):
    _cast_weight_once(w_ref, wb_ref, 0)
    x = r_ref[...] + _dot(a_ref[...], wb_ref[...])
    x_ref[...] = x
    h_ref[...] = _rms_rows(x, g_ref[...]).astype(h_ref.dtype)


def _out_proj(a, w, r, g, tm):
    t, k = a.shape
    d = w.shape[1]
    return pl.pallas_call(
        _out_proj_kernel,
        grid=(t // tm,),
        in_specs=[pl.BlockSpec((tm, k), lambda i: (i, 0)),
                  pl.BlockSpec((k, d), lambda i: (0, 0)),
                  pl.BlockSpec((tm, d), lambda i: (i, 0)),
                  pl.BlockSpec((1, d), lambda i: (0, 0))],
        out_specs=[pl.BlockSpec((tm, d), lambda i: (i, 0)),
                   pl.BlockSpec((tm, d), lambda i: (i, 0))],
        out_shape=[jax.ShapeDtypeStruct((t, d), F32), jax.ShapeDtypeStruct((t, d), BF16)],
        scratch_shapes=[pltpu.VMEM((k, d), BF16)],
        compiler_params=_params(("arbitrary",)),
        name="out_proj",
    )(a, w, r, g.reshape(1, d))


def _ffn_in_kernel(h_ref, wg_ref, wu_ref, o_ref, wgb_ref, wub_ref):
    _cast_weight_once(wg_ref, wgb_ref, 1)
    _cast_weight_once(wu_ref, wub_ref, 1)
    half = h_ref.shape[0] // 2
    for rows in (slice(0, half), slice(half, 2 * half)):
        h = h_ref[rows, :]
        gate = _dot(h, wgb_ref[...])
        up = _dot(h, wub_ref[...])
        o_ref[rows, :] = (gate * _sigmoid(gate) * up).astype(o_ref.dtype)


def _ffn_in(h, w, hidden, tm, tn):
    t, d = h.shape
    tn = _tile(hidden, tn)
    nb = hidden // tn
    return pl.pallas_call(
        _ffn_in_kernel,
        grid=(nb, t // tm),
        in_specs=[pl.BlockSpec((tm, d), lambda j, i: (i, 0)),
                  pl.BlockSpec((d, tn), lambda j, i: (0, j)),
                  pl.BlockSpec((d, tn), lambda j, i: (0, j + nb))],
        out_specs=pl.BlockSpec((tm, tn), lambda j, i: (i, j)),
        out_shape=jax.ShapeDtypeStruct((t, hidden), BF16),
        scratch_shapes=[pltpu.VMEM((d, tn), BF16), pltpu.VMEM((d, tn), BF16)],
        compiler_params=_params(("parallel", "arbitrary")),
        name="swiglu_in",
    )(h, w, w)


def _matmul_residual_kernel(a_hbm, w_ref, r_ref, o_ref, a_buf, a_sem, *, tm):
    i = pl.program_id(0)
    j = pl.program_id(1)

    def fetch(row_step):
        slot = row_step % 2
        return pltpu.make_async_copy(a_hbm.at[pl.ds(row_step * tm, tm), :], a_buf.at[slot],
                                     a_sem.at[slot])

    @pl.when(jnp.logical_and(i == 0, j == 0))
    def _():
        fetch(0).start()

    @pl.when(j == 0)
    def _():
        @pl.when(i + 1 < pl.num_programs(0))
        def _():
            fetch(i + 1).start()
        fetch(i).wait()

    o_ref[...] = r_ref[...] + _dot(a_buf[i % 2], w_ref[...])


def _matmul_residual(a, w, r, tm, tn, name):
    t, k = a.shape
    d = w.shape[1]
    tn = _tile(d, tn)
    return pl.pallas_call(
        functools.partial(_matmul_residual_kernel, tm=tm),
        grid=(t // tm, d // tn),
        in_specs=[pl.BlockSpec(memory_space=pl.ANY),
                  pl.BlockSpec((k, tn), lambda i, j: (0, j)),
                  pl.BlockSpec((tm, tn), lambda i, j: (i, j))],
        out_specs=pl.BlockSpec((tm, tn), lambda i, j: (i, j)),
        out_shape=jax.ShapeDtypeStruct((t, d), F32),
        scratch_shapes=[pltpu.VMEM((2, tm, k), a.dtype), pltpu.SemaphoreType.DMA((2,))],
        compiler_params=_params(("arbitrary", "arbitrary")),
        name=name,
    )(a, w, r)


def _ple_kernel(x_ref, g_ref, wg_ref, p_ref, wp_ref, o_ref):
    half = x_ref.shape[0] // 2
    for rows in (slice(0, half), slice(half, 2 * half)):
        x = x_ref[rows, :]
        r = lax.rsqrt(jnp.mean(x * x, axis=-1, keepdims=True) + EPS)
        gate = jax.nn.sigmoid(r * _dot((x * g_ref[...]).astype(BF16), wg_ref[...]))
        ple = _dot(p_ref[rows, :].astype(BF16), wp_ref[...])
        o_ref[rows, :] = x + gate * ple


def _ple(x, g, wg, p, wp, tm):
    t, d = x.shape
    pd = p.shape[1]
    return pl.pallas_call(
        _ple_kernel,
        grid=(t // tm,),
        in_specs=[pl.BlockSpec((tm, d), lambda i: (i, 0)),
                  pl.BlockSpec((1, d), lambda i: (0, 0)),
                  pl.BlockSpec((d, d), lambda i: (0, 0)),
                  pl.BlockSpec((tm, pd), lambda i: (i, 0)),
                  pl.BlockSpec((pd, d), lambda i: (0, 0))],
        out_specs=pl.BlockSpec((tm, d), lambda i: (i, 0)),
        out_shape=jax.ShapeDtypeStruct((t, d), F32),
        compiler_params=_params(("parallel",)),
        name="gated_ple",
    )(x, g.reshape(1, d), wg, p, wp)


def _layer(x, p, attn_norm_g, w_in, sgu_norm_g, w_s, b_s, q_norm_g, k_norm_g,
           w_up_a, w_up_b, w_o, ffn_norm_g, w_ffn_in, w_ffn_out,
           ple_norm_g, w_ple_gate, w_ple, batch, seq):
    t, d = x.shape
    hidden = w_ffn_out.shape[0]
    tm = _tile(t, 1024)
    tm_full = _tile(t, 512)
    tm_big = _tile(t, 2048)
    blk = _tile(seq, 256)
    nsub = next(n for n in (16, 8, 4, 2, 1) if seq % (n * blk) == 0)

    c_qk = 2 * WIDTH
    c_v = c_qk + 2 * WIDTH
    c_gate = c_v + WIDTH

    uv, h = _proj_sgu(x, attn_norm_g, w_in, c_qk, tm_full)
    scale = GROUP ** -0.5 * LOG2_E
    qk_gain = jnp.concatenate([jnp.tile(q_norm_g * scale, N_GROUPS),
                               jnp.tile(k_norm_g, N_GROUPS)]).reshape(1, 2 * WIDTH)
    qk = _proj(h, w_in, c_qk, 2 * WIDTH, "headnorm", qk_gain, tm, 2048, "proj_qk")
    v_att = _proj(h, w_in, c_v, WIDTH, "none", None, tm_big, 1024, "proj_v")
    gates = _proj(h, w_in, c_gate, 2 * d, "sigmoid", None, tm_big, 1024, "proj_gates")

    y_b = _attention(qk, v_att, batch, seq, blk, nsub)

    merged = _merge(uv, sgu_norm_g.reshape(-1), w_s, b_s, y_b,
                    w_up_a, w_up_b, gates, tm_full)
    x, h = _out_proj(merged, w_o, x, ffn_norm_g, tm_full)

    act = _ffn_in(h, w_ffn_in, hidden, tm_big, 512)
    x = _matmul_residual(act, w_ffn_out.astype(BF16), x, tm, 512, "swiglu_out")

    return _ple(x, ple_norm_g, w_ple_gate.astype(BF16), p, w_ple.astype(BF16), tm)


def kernel(x, p, attn_norm_g, w_in, sgu_norm_g, w_s, b_s, q_norm_g, k_norm_g, w_up_a, w_up_b, w_o,
           ffn_norm_g, w_ffn_in, w_ffn_out, ple_norm_g, w_ple_gate, w_ple):
    batch, seq, d = x.shape
    xf = x.reshape(batch * seq, d)
    for i in range(p.shape[0]):
        xf = _layer(xf, p[i].reshape(batch * seq, -1), attn_norm_g[i], w_in[i], sgu_norm_g[i], w_s[i],
                    b_s[i], q_norm_g[i], k_norm_g[i], w_up_a[i], w_up_b[i], w_o[i], ffn_norm_g[i],
                    w_ffn_in[i], w_ffn_out[i], ple_norm_g[i], w_ple_gate[i], w_ple[i], batch, seq)
    return xf.reshape(batch, seq, d)
```
